```python
import math, functools
import jax, jax.numpy as jnp
from jax import lax
import numpy as np

D_MODEL = 1024
BATCH = 8
SEQ = 4096
DEPTH = 4

CTX_LEN = 256
GRID_W = 64

N_MOD = 9
D_FF = 2816
MACARON = 0.5
ALPHA = (2.0 * DEPTH) ** 0.25
BETA = (8.0 * DEPTH) ** -0.25
LN_EPS = 1e-5

S5_WIDTH = D_MODEL // 4
S5_GROUP = 16
S5_GROUPS = S5_WIDTH // S5_GROUP
S5_STATE = 64

RET_HEAD_DIM = 128
RET_WIDTH = D_MODEL - S5_WIDTH
RET_HEADS = RET_WIDTH // RET_HEAD_DIM
RET_CHUNK = 128
ROPE_BASE = 10000.0

GDN_WIDTH = D_MODEL // 2
GDN_HEAD_DIM = 128
GDN_HEADS = GDN_WIDTH // GDN_HEAD_DIM
GDN_CONV = 5
GDN_CHUNK = 64

RWKV_WIDTH = D_MODEL // 2
RWKV_HEAD_DIM = 64
RWKV_HEADS = RWKV_WIDTH // RWKV_HEAD_DIM
RWKV_DECAY_LORA = 32
RWKV_AAA_LORA = 32
RWKV_GATE_LORA = 96
RWKV_GN_EPS = 64e-5

AB_IN = S5_WIDTH + 4 * RET_WIDTH
AB_SPLIT = [S5_WIDTH, S5_WIDTH + RET_WIDTH, S5_WIDTH + 2 * RET_WIDTH, S5_WIDTH + 3 * RET_WIDTH]
GDN_IN = 4 * GDN_WIDTH + 4 * GDN_HEADS
GDN_SPLIT = [3 * GDN_WIDTH, 4 * GDN_WIDTH, 4 * GDN_WIDTH + 2 * GDN_HEADS]
RWKV_IN = 3 * RWKV_WIDTH + 2 * RWKV_DECAY_LORA + 2 * RWKV_AAA_LORA + RWKV_GATE_LORA
RWKV_SPLIT = [RWKV_WIDTH, 2 * RWKV_WIDTH, 3 * RWKV_WIDTH,
              3 * RWKV_WIDTH + 2 * RWKV_DECAY_LORA,
              3 * RWKV_WIDTH + 2 * RWKV_DECAY_LORA + 2 * RWKV_AAA_LORA]
CD_IN = GDN_IN + RWKV_IN
N_AB = (DEPTH + 1) // 2
N_CD = DEPTH // 2

kernel_name = "hybrid_s5_retnet_gdn_rwkv7_flow_trunk"


def layer_norm(x, g, b):
    xf = x.astype(jnp.float32)
    mu = jnp.mean(xf, -1, keepdims=True)
    var = jnp.mean(jnp.square(xf - mu), -1, keepdims=True)
    return ((xf - mu) * lax.rsqrt(var + LN_EPS) * g + b).astype(x.dtype)


def head_norm(x, eps):
    mu = jnp.mean(x, -1, keepdims=True)
    var = jnp.mean(jnp.square(x - mu), -1, keepdims=True)
    return (x - mu) * lax.rsqrt(var + eps)


def l2norm(x, eps=1e-6):
    return x * lax.rsqrt(jnp.sum(x * x, -1, keepdims=True) + eps)


def modulate(h, shift, scale):
    return h * (1.0 + scale) + shift


def swiglu(h, w_in, w_out):
    a, b = jnp.split(h @ w_in, 2, axis=-1)
    return (jax.nn.silu(a) * b) @ w_out


def post_norm(h, y, gate, g, b):
    return layer_norm(ALPHA * h + gate * y, g, b)


def ffn_half_step(h, shift, scale, gate, w_in, w_out, g, b):
    return post_norm(h, MACARON * swiglu(modulate(h, shift, scale), w_in, w_out), gate, g, b)


def centred_dwconv(x, w):
    k = w.shape[0]
    return lax.conv_general_dilated(x, w[:, None, :].astype(x.dtype), (1,), [(k // 2, k // 2)],
                                    dimension_numbers=("NWC", "WIO", "NWC"),
                                    feature_group_count=x.shape[-1])


def sym_shift(x):
    xp = jnp.pad(x, ((0, 0), (1, 1), (0, 0)))
    return 0.5 * (xp[:, :-2] + xp[:, 2:])


def axial_rope(x):
    n, d = x.shape[1], x.shape[-1]
    rows = n // GRID_W
    nf = d // 4
    freqs = ROPE_BASE ** (-jnp.arange(nf, dtype=jnp.float32) / nf)
    pr = jnp.broadcast_to(jnp.arange(rows, dtype=jnp.float32)[:, None], (rows, GRID_W)).reshape(-1)
    pc = jnp.broadcast_to(jnp.arange(GRID_W, dtype=jnp.float32)[None, :], (rows, GRID_W)).reshape(-1)
    ang = jnp.concatenate([pr[:, None] * freqs, pc[:, None] * freqs], -1)
    cos, sin = jnp.cos(ang)[None, :, None, :], jnp.sin(ang)[None, :, None, :]
    x1, x2 = jnp.split(x.astype(jnp.float32), 2, axis=-1)
    return jnp.concatenate([x1 * cos - x2 * sin, x1 * sin + x2 * cos], -1)


def bidirectional(scan_fns, ctx_dirs, lat_dirs, state0):
    y_ctx, y_lat = 0.0, 0.0
    for d in range(2):
        cs, ls = ctx_dirs[d], lat_dirs[d]
        if d == 1:
            cs = tuple(jnp.flip(t, 1) for t in cs)
            ls = tuple(jnp.flip(t, 1) for t in ls)
        yc, s_ctx = scan_fns[d](cs, state0)
        yl, _ = scan_fns[d](ls, s_ctx)
        if d == 1:
            yc, yl = jnp.flip(yc, 1), jnp.flip(yl, 1)
        y_ctx = y_ctx + yc
        y_lat = y_lat + yl
    return y_ctx, y_lat


def s5_discretise(lam_re, lam_im, log_dt, b_re, b_im):
    lam_re, lam_im = lam_re.astype(jnp.float32), lam_im.astype(jnp.float32)
    dt = jnp.exp(log_dt.astype(jnp.float32))[:, None]
    mag, ang = jnp.exp(lam_re * dt), lam_im * dt
    lb_re, lb_im = mag * jnp.cos(ang), mag * jnp.sin(ang)
    den = lam_re * lam_re + lam_im * lam_im
    nr = lb_re - 1.0
    coef_re = (nr * lam_re + lb_im * lam_im) / den
    coef_im = (lb_im * lam_re - nr * lam_im) / den
    b_re, b_im = b_re.astype(jnp.float32), b_im.astype(jnp.float32)
    bb_re = coef_re[..., None] * b_re - coef_im[..., None] * b_im
    bb_im = coef_re[..., None] * b_im + coef_im[..., None] * b_re
    return lb_re, lb_im, bb_re, bb_im


def complex_affine_combine(e1, e2):
    a1r, a1i, b1r, b1i = e1
    a2r, a2i, b2r, b2i = e2
    return (a1r * a2r - a1i * a2i, a1r * a2i + a1i * a2r,
            a2r * b1r - a2i * b1i + b2r, a2r * b1i + a2i * b1r + b2i)


def s5_scan(seq, state0, lam_bar_re, lam_bar_im, c_re, c_im):
    bu_re, bu_im = seq
    n = bu_re.shape[1]
    a_re = jnp.broadcast_to(lam_bar_re, (1, n) + lam_bar_re.shape)
    a_im = jnp.broadcast_to(lam_bar_im, (1, n) + lam_bar_im.shape)
    acc_re, acc_im, h_re, h_im = lax.associative_scan(
        complex_affine_combine, (a_re, a_im, bu_re, bu_im), axis=1)
    h0_re, h0_im = state0[0][:, None], state0[1][:, None]
    h_re, h_im = (h_re + acc_re * h0_re - acc_im * h0_im,
                  h_im + acc_re * h0_im + acc_im * h0_re)
    y = (jnp.einsum("blgp,gcp->blgc", h_re, c_re)
         - jnp.einsum("blgp,gcp->blgc", h_im, c_im))
    return y, (h_re[:, -1], h_im[:, -1])


def s5_drive(gu, bb_re, bb_im):
    return (jnp.einsum("blgc,gpc->blgp", gu, bb_re), jnp.einsum("blgc,gpc->blgp", gu, bb_im))


def s5_branch(u_c, u_l, lam_re, lam_im, log_dt, b_re, b_im, c_re, c_im, d_skip, glu_w, glu_b):
    def groups(u):
        return u.astype(jnp.float32).reshape(u.shape[0], u.shape[1], S5_GROUPS, S5_GROUP)
    gu_c, gu_l = groups(u_c), groups(u_l)
    c_re, c_im = c_re.astype(jnp.float32), c_im.astype(jnp.float32)
    scan_fns, ctx_dirs, lat_dirs = [], [], []
    for d in range(2):
        lb_re, lb_im, bb_re, bb_im = s5_discretise(lam_re[d], lam_im[d], log_dt[d], b_re, b_im)
        scan_fns.append(functools.partial(s5_scan, lam_bar_re=lb_re, lam_bar_im=lb_im,
                                          c_re=c_re, c_im=c_im))
        ctx_dirs.append(s5_drive(gu_c, bb_re, bb_im))
        lat_dirs.append(s5_drive(gu_l, bb_re, bb_im))
    zeros = jnp.zeros((u_c.shape[0], S5_GROUPS, S5_STATE), jnp.float32)
    y_c, y_l = bidirectional(scan_fns, ctx_dirs, lat_dirs, (zeros, zeros))

    def finish(y, gu):
        y = (y + d_skip.reshape(S5_GROUPS, S5_GROUP) * gu).reshape(gu.shape[0], gu.shape[1], S5_WIDTH)
        z = jax.nn.gelu(y)
        return z * jax.nn.sigmoid(z @ glu_w + glu_b)
    return finish(y_c, gu_c), finish(y_l, gu_l)


def chunk_retention(seq, state0, log_gamma):
    q, k, v = seq
    b, n, h, _ = q.shape
    cs = RET_CHUNK
    nc = n // cs
    q, k, v = (t.reshape(b, nc, cs, h, -1) for t in (q, k, v))
    idx = jnp.arange(cs, dtype=jnp.float32)
    diff = idx[:, None] - idx[None, :]
    past = diff >= 0
    decay = jnp.where(past, jnp.exp(jnp.where(past, diff, 0.0)[None] * log_gamma[:, None, None]), 0.0)
    scores = jnp.einsum("bnihd,bnjhd->bnhij", q, k) * decay
    intra = jnp.einsum("bnhij,bnjhe->bnihe", scores, v)
    zeta = jnp.exp((cs - 1.0 - idx)[:, None] * log_gamma)
    kv = jnp.einsum("bnjhd,jh,bnjhe->nbhde", k, zeta, v)
    chunk_decay = jnp.exp(cs * log_gamma)[:, None, None]

    def step(s, kv_n):
        return chunk_decay * s + kv_n, s
    s_final, s_prev = lax.scan(step, state0, kv)
    xi = jnp.exp((idx + 1.0)[:, None] * log_gamma)
    inter = jnp.einsum("bnihd,ih,nbhde->bnihe", q, xi, s_prev)
    return (intra + inter).reshape(b, n, h, -1), s_final


def retention_branch(p_c, p_l, log_rate):
    scale = RET_HEAD_DIM ** -0.5

    def heads(t):
        return t.astype(jnp.float32).reshape(t.shape[0], t.shape[1], RET_HEADS, RET_HEAD_DIM)
    q_c, k_c, v_c, g_c = p_c
    q_l, k_l, v_l, g_l = p_l
    seq_c = (heads(q_c), heads(k_c) * scale, heads(v_c))
    seq_l = (axial_rope(heads(q_l)), axial_rope(heads(k_l)) * scale, heads(v_l))
    log_gamma = -jnp.exp(log_rate.astype(jnp.float32))
    scan_fns = [functools.partial(chunk_retention, log_gamma=log_gamma[d]) for d in range(2)]
    state0 = jnp.zeros((q_c.shape[0], RET_HEADS, RET_HEAD_DIM, RET_HEAD_DIM), jnp.float32)
    o_c, o_l = bidirectional(scan_fns, (seq_c, seq_c), (seq_l, seq_l), state0)

    def finish(o, g):
        return jax.nn.silu(g.astype(jnp.float32)) * head_norm(o, LN_EPS).reshape(g.shape)
    return finish(o_c, g_c), finish(o_l, g_l)


def s5_retention_mixer(h_c, h_l, w_in, w_out, lam_re, lam_im, log_dt, b_re, b_im, c_re, c_im,
                       d_skip, glu_w, glu_b, ret_log_rate):
    u_c, q_c, k_c, v_c, g_c = jnp.split(h_c @ w_in, AB_SPLIT, axis=-1)
    u_l, q_l, k_l, v_l, g_l = jnp.split(h_l @ w_in, AB_SPLIT, axis=-1)
    a_c, a_l = s5_branch(u_c, u_l, lam_re, lam_im, log_dt, b_re, b_im, c_re, c_im, d_skip, glu_w, glu_b)
    r_c, r_l = retention_branch((q_c, k_c, v_c, g_c), (q_l, k_l, v_l, g_l), ret_log_rate)
    y_c = jnp.concatenate([a_c, r_c], -1).astype(h_c.dtype) @ w_out
    y_l = jnp.concatenate([a_l, r_l], -1).astype(h_l.dtype) @ w_out
    return y_c, y_l


def to_blocks(t, chunk):
    b, n = t.shape[:2]
    return jnp.moveaxis(t.reshape((b, n // chunk, chunk) + t.shape[2:]), 3, 2)


def chunk_gated_delta(seq, state0):
    q, k, v, g, beta = seq
    b, n, h, _ = q.shape
    cs = GDN_CHUNK
    q, k, v, g, beta = (to_blocks(t, cs) for t in (q, k, v, g, beta))
    gcum = jnp.cumsum(g, axis=-1)
    idx = jnp.arange(cs)
    incl = idx[:, None] >= idx[None, :]
    strict = idx[:, None] > idx[None, :]
    decay = jnp.where(incl, jnp.exp(jnp.where(incl, gcum[..., :, None] - gcum[..., None, :], 0.0)), 0.0)
    kb = k * beta[..., None]
    a_mat = jnp.where(strict, jnp.einsum("bnhid,bnhjd->bnhij", kb, k) * decay, 0.0)
    eye = jnp.eye(cs, dtype=jnp.float32)
    t_mat = lax.linalg.triangular_solve(a_mat + eye, jnp.broadcast_to(eye, a_mat.shape),
                                        left_side=True, lower=True, unit_diagonal=True)
    w = t_mat @ (kb * jnp.exp(gcum)[..., None])
    u = t_mat @ (v * beta[..., None])
    attn = jnp.einsum("bnhid,bnhjd->bnhij", q, k) * decay
    q_dec = q * jnp.exp(gcum)[..., None]
    k_dec = k * jnp.exp(gcum[..., -1:] - gcum)[..., None]
    g_last = jnp.exp(gcum[..., -1])[..., None, None]

    def step(s, blk):
        w_n, u_n, attn_n, qd_n, kd_n, gl_n = blk
        v_new = u_n - w_n @ s
        o_n = qd_n @ s + attn_n @ v_new
        s = s * gl_n + jnp.swapaxes(kd_n, -1, -2) @ v_new
        return s, o_n
    s_final, o = lax.scan(step, state0,
                          tuple(jnp.moveaxis(t, 1, 0) for t in (w, u, attn, q_dec, k_dec, g_last)))
    o = jnp.moveaxis(jnp.moveaxis(o, 0, 1), 3, 2)
    return o.reshape(b, n, h, -1), s_final


def deltanet_branch(p_c, p_l, conv_w, a_log, dt_bias, norm_w):
    a_log, dt_bias = a_log.astype(jnp.float32), dt_bias.astype(jnp.float32)

    def prep(p):
        bsz, n = p.shape[:2]
        qkv, z, a_in, b_in = jnp.split(p, GDN_SPLIT, axis=-1)
        qkv = jax.nn.silu(centred_dwconv(qkv, conv_w).astype(jnp.float32))
        q, k, v = (t.reshape(bsz, n, GDN_HEADS, GDN_HEAD_DIM) for t in jnp.split(qkv, 3, axis=-1))
        q = l2norm(q) * GDN_HEAD_DIM ** -0.5
        k = l2norm(k)
        a_in = a_in.astype(jnp.float32).reshape(bsz, n, 2, GDN_HEADS)
        beta = jax.nn.sigmoid(b_in.astype(jnp.float32)).reshape(bsz, n, 2, GDN_HEADS)
        g = -jnp.exp(a_log) * jax.nn.softplus(a_in + dt_bias)
        return [(q, k, v, g[:, :, d], beta[:, :, d]) for d in range(2)], z
    dirs_c, z_c = prep(p_c)
    dirs_l, z_l = prep(p_l)
    state0 = jnp.zeros((p_c.shape[0], GDN_HEADS, GDN_HEAD_DIM, GDN_HEAD_DIM), jnp.float32)
    o_c, o_l = bidirectional((chunk_gated_delta, chunk_gated_delta), dirs_c, dirs_l, state0)

    def finish(o, z):
        o = o * lax.rsqrt(jnp.mean(o * o, -1, keepdims=True) + 1e-6) * norm_w
        return o.reshape(z.shape) * jax.nn.silu(z.astype(jnp.float32))
    return finish(o_c, z_c), finish(o_l, z_l)


def rwkv7_scan(seq, state0):
    seq_t = tuple(jnp.moveaxis(t, 1, 0) for t in seq)

    def step(s, inp):
        r, w, k, v, a, bb = inp
        sa = jnp.einsum("bhvk,bhk->bhv", s, a)
        s = s * w[:, :, None, :] + sa[..., None] * bb[:, :, None, :] + v[..., None] * k[:, :, None, :]
        return s, jnp.einsum("bhvk,bhk->bhv", s, r)
    s_final, y = lax.scan(step, state0, seq_t)
    return jnp.moveaxis(y, 0, 1), s_final


def rwkv7_branch(p_c, p_l, mu, w0, w_up, a0, a_up, g_up, k_k, k_a, r_k, ln_w, ln_b):
    def prep(p):
        bsz, n = p.shape[:2]
        p = p.astype(jnp.float32)
        p = p + mu * (sym_shift(p) - p)
        r, k, v, wd, ad, gd = jnp.split(p, RWKV_SPLIT, axis=-1)

        def hd(t):
            return t.reshape(bsz, n, RWKV_HEADS, RWKV_HEAD_DIM)
        kk = l2norm(hd(k * k_k))
        g = jax.nn.sigmoid(gd) @ g_up
        wd = wd.reshape(bsz, n, 2, RWKV_DECAY_LORA)
        ad = ad.reshape(bsz, n, 2, RWKV_AAA_LORA)
        dirs = []
        for d in range(2):
            w_log = -jax.nn.softplus(-(w0[d] + jnp.tanh(wd[:, :, d]) @ w_up[d])) - 0.5
            a = jax.nn.sigmoid(a0[d] + ad[:, :, d] @ a_up[d])
            k_d = k * (1.0 + (a - 1.0) * k_a)
            dirs.append((hd(r), hd(jnp.exp(-jnp.exp(w_log))), hd(k_d), hd(v), -kk, kk * hd(a)))
        return dirs, g
    dirs_c, g_c = prep(p_c)
    dirs_l, g_l = prep(p_l)
    state0 = jnp.zeros((p_c.shape[0], RWKV_HEADS, RWKV_HEAD_DIM, RWKV_HEAD_DIM), jnp.float32)
    y_c, y_l = bidirectional((rwkv7_scan, rwkv7_scan), dirs_c, dirs_l, state0)

    def finish(y, dirs, g):
        bsz, n = y.shape[:2]
        y = head_norm(y, RWKV_GN_EPS).reshape(bsz, n, RWKV_WIDTH) * ln_w + ln_b
        r, v = dirs[0][0], dirs[0][3]
        bonus = (jnp.sum(r * dirs[0][2] * r_k, -1, keepdims=True)
                 + jnp.sum(r * dirs[1][2] * r_k, -1, keepdims=True)) * v
        return (y + bonus.reshape(bsz, n, RWKV_WIDTH)) * g
    return finish(y_c, dirs_c, g_c), finish(y_l, dirs_l, g_l)


def deltanet_rwkv_mixer(h_c, h_l, w_in, w_out, conv_w, a_log, dt_bias, norm_w, mu, w0, w_up,
                        a0, a_up, g_up, k_k, k_a, r_k, ln_w, ln_b):
    p_c, p_l = h_c @ w_in, h_l @ w_in
    d_c, d_l = deltanet_branch(p_c[..., :GDN_IN], p_l[..., :GDN_IN], conv_w, a_log, dt_bias, norm_w)
    r_c, r_l = rwkv7_branch(p_c[..., GDN_IN:], p_l[..., GDN_IN:], mu, w0, w_up, a0, a_up, g_up,
                            k_k, k_a, r_k, ln_w, ln_b)
    y_c = jnp.concatenate([d_c, r_c], -1).astype(h_c.dtype) @ w_out
    y_l = jnp.concatenate([d_l, r_l], -1).astype(h_l.dtype) @ w_out
    return y_c, y_l


def setup_inputs(seed: int = 0) -> dict:
    key = jax.random.key(seed)
    ks = jax.random.split(key, 48)
    f32 = jnp.float32

    def nrm(i, shape, std):
        return std * jax.random.normal(ks[i], shape, f32)

    def unif(i, shape, lo, hi):
        return jax.random.uniform(ks[i], shape, f32, lo, hi)
    D, G, P, W = D_MODEL, S5_GROUPS, S5_STATE, RWKV_WIDTH
    ret_base = jnp.log(-jnp.log(1.0 - 2.0 ** (-5.0 - jnp.arange(RET_HEADS, dtype=f32))))
    gdn_dt = jnp.exp(unif(27, (N_CD, 2, GDN_HEADS), math.log(1e-3), math.log(1e-1)))
    w0_base = -6.0 + 5.0 * jnp.linspace(0.0, 1.0, W, dtype=f32) ** 1.35
    return {
        "x": nrm(0, (BATCH, SEQ, D), 1.0),
        "c": nrm(1, (BATCH, D), 1.0),
        "ctx": nrm(2, (BATCH, CTX_LEN, D), 1.0),
        "c_ctx": nrm(3, (D,), 1.0),
        "ada_w": nrm(4, (DEPTH, D, N_MOD * D), D ** -0.5),
        "ada_b": nrm(5, (DEPTH, N_MOD * D), 0.02),
        "ffn_w_in": nrm(6, (DEPTH, 2, D, 2 * D_FF), D ** -0.5),
        "ffn_w_out": nrm(7, (DEPTH, 2, D_FF, D), BETA * D_FF ** -0.5),
        "ln_g": 1.0 + nrm(8, (DEPTH, 3, D), 0.02),
        "ln_b": nrm(9, (DEPTH, 3, D), 0.02),
        "ab_w_in": nrm(10, (N_AB, D, AB_IN), D ** -0.5),
        "ab_w_out": nrm(11, (N_AB, S5_WIDTH + RET_WIDTH, D), BETA * D ** -0.5),
        "s5_lam_re": -0.5 + nrm(12, (N_AB, 2, G, P), 0.01),
        "s5_lam_im": math.pi * jnp.arange(P, dtype=f32) + nrm(13, (N_AB, 2, G, P), 0.01),
        "s5_log_dt": unif(14, (N_AB, 2, G), math.log(1e-3), math.log(1e-1)),
        "s5_b_re": nrm(15, (N_AB, G, P, S5_GROUP), S5_GROUP ** -0.5),
        "s5_b_im": nrm(16, (N_AB, G, P, S5_GROUP), S5_GROUP ** -0.5),
        "s5_c_re": nrm(17, (N_AB, G, S5_GROUP, P), P ** -0.5),
        "s5_c_im": nrm(18, (N_AB, G, S5_GROUP, P), P ** -0.5),
        "s5_d": nrm(19, (N_AB, S5_WIDTH), 1.0),
        "s5_glu_w": nrm(20, (N_AB, S5_WIDTH, S5_WIDTH), S5_WIDTH ** -0.5),
        "s5_glu_b": nrm(21, (N_AB, S5_WIDTH), 0.02),
        "ret_log_rate": ret_base + nrm(22, (N_AB, 2, RET_HEADS), 0.05),
        "cd_w_in": nrm(23, (N_CD, D, CD_IN), D ** -0.5),
        "cd_w_out": nrm(24, (N_CD, GDN_WIDTH + RWKV_WIDTH, D), BETA * D ** -0.5),
        "gdn_conv_w": nrm(25, (N_CD, GDN_CONV, 3 * GDN_WIDTH), GDN_CONV ** -0.5),
        "gdn_a_log": jnp.log(unif(26, (N_CD, 2, GDN_HEADS), 1.0, 16.0)),
        "gdn_dt_bias": gdn_dt + jnp.log(-jnp.expm1(-gdn_dt)),
        "gdn_norm_w": 1.0 + nrm(28, (N_CD, GDN_HEAD_DIM), 0.02),
        "rwkv_mu": unif(29, (N_CD, RWKV_IN), 0.0, 1.0),
        "rwkv_w0": w0_base + nrm(30, (N_CD, 2, W), 0.1),
        "rwkv_w_up": nrm(31, (N_CD, 2, RWKV_DECAY_LORA, W), 0.1 * RWKV_DECAY_LORA ** -0.5),
        "rwkv_a0": nrm(32, (N_CD, 2, W), 0.1),
        "rwkv_a_up": nrm(33, (N_CD, 2, RWKV_AAA_LORA, W), RWKV_AAA_LORA ** -0.5),
        "rwkv_g_up": nrm(34, (N_CD, RWKV_GATE_LORA, W), RWKV_GATE_LORA ** -0.5),
        "rwkv_k_k": 0.85 + nrm(35, (N_CD, W), 0.02),
        "rwkv_k_a": 1.0 + nrm(36, (N_CD, W), 0.02),
        "rwkv_r_k": nrm(37, (N_CD, RWKV_HEADS, RWKV_HEAD_DIM), 0.1),
        "rwkv_ln_w": 1.0 + nrm(38, (N_CD, W), 0.02),
        "rwkv_ln_b": nrm(39, (N_CD, W), 0.02),
    }


def reference(x, c, ctx, c_ctx, ada_w, ada_b, ffn_w_in, ffn_w_out, ln_g, ln_b,
              ab_w_in, ab_w_out, s5_lam_re, s5_lam_im, s5_log_dt, s5_b_re, s5_b_im, s5_c_re, s5_c_im,
              s5_d, s5_glu_w, s5_glu_b, ret_log_rate,
              cd_w_in, cd_w_out, gdn_conv_w, gdn_a_log, gdn_dt_bias, gdn_norm_w,
              rwkv_mu, rwkv_w0, rwkv_w_up, rwkv_a0, rwkv_a_up, rwkv_g_up, rwkv_k_k, rwkv_k_a, rwkv_r_k,
              rwkv_ln_w, rwkv_ln_b):
    h_lat, h_ctx = x, ctx
    s_lat, s_ctx = jax.nn.silu(c), jax.nn.silu(c_ctx)
    for i in range(DEPTH):
        m_lat = (s_lat @ ada_w[i] + ada_b[i]).reshape(-1, N_MOD, 1, D_MODEL)
        m_ctx = (s_ctx @ ada_w[i] + ada_b[i]).reshape(1, N_MOD, 1, D_MODEL)
        ml = [m_lat[:, j] for j in range(N_MOD)]
        mc = [m_ctx[:, j] for j in range(N_MOD)]

        h_lat = ffn_half_step(h_lat, ml[0], ml[1], ml[2], ffn_w_in[i, 0], ffn_w_out[i, 0], ln_g[i, 0], ln_b[i, 0])
        h_ctx = ffn_half_step(h_ctx, mc[0], mc[1], mc[2], ffn_w_in[i, 0], ffn_w_out[i, 0], ln_g[i, 0], ln_b[i, 0])

        x_lat = modulate(h_lat, ml[3], ml[4])
        x_ctx = modulate(h_ctx, mc[3], mc[4])
        j = i // 2
        if i % 2 == 0:
            y_ctx, y_lat = s5_retention_mixer(
                x_ctx, x_lat, ab_w_in[j], ab_w_out[j], s5_lam_re[j], s5_lam_im[j], s5_log_dt[j],
                s5_b_re[j], s5_b_im[j], s5_c_re[j], s5_c_im[j], s5_d[j], s5_glu_w[j], s5_glu_b[j],
                ret_log_rate[j])
        else:
            y_ctx, y_lat = deltanet_rwkv_mixer(
                x_ctx, x_lat, cd_w_in[j], cd_w_out[j], gdn_conv_w[j], gdn_a_log[j], gdn_dt_bias[j],
                gdn_norm_w[j], rwkv_mu[j], rwkv_w0[j], rwkv_w_up[j], rwkv_a0[j], rwkv_a_up[j],
                rwkv_g_up[j], rwkv_k_k[j], rwkv_k_a[j], rwkv_r_k[j], rwkv_ln_w[j], rwkv_ln_b[j])
        h_lat = post_norm(h_lat, y_lat, ml[5], ln_g[i, 1], ln_b[i, 1])

        h_lat = ffn_half_step(h_lat, ml[6], ml[7], ml[8], ffn_w_in[i, 1], ffn_w_out[i, 1], ln_g[i, 2], ln_b[i, 2])
        if i < DEPTH - 1:
            h_ctx = post_norm(h_ctx, y_ctx, mc[5], ln_g[i, 1], ln_b[i, 1])
            h_ctx = ffn_half_step(h_ctx, mc[6], mc[7], mc[8], ffn_w_in[i, 1], ffn_w_out[i, 1], ln_g[i, 2], ln_b[i, 2])
    return h_lat
```

```python
import functools
import math

import jax
import jax.numpy as jnp
from jax import lax
from jax.experimental import pallas as pl
from jax.experimental.pallas import tpu as pltpu

F32 = jnp.float32
BF16 = jnp.bfloat16
HI = lax.Precision.HIGHEST

D_MODEL = 1024
DEPTH = 4
N_MOD = 9
D_FF = 2816
MACARON = 0.5
ALPHA = (2.0 * DEPTH) ** 0.25
LN_EPS = 1e-5
GRID_W = 64

LANE = 128
SUBLANE = 8
MXU_TILE = 256
VMEM_LIMIT_BYTES = 56 * 1024 * 1024

TOKEN_ROWS = 512
FF_CHUNK = MXU_TILE


def _cparams(*sem):
    return pltpu.CompilerParams(dimension_semantics=sem, vmem_limit_bytes=VMEM_LIMIT_BYTES)


def _full_spec(a):
    nd = a.ndim
    return pl.BlockSpec(a.shape, lambda *_, nd=nd: (0,) * nd)


def _dot(a, b):
    return jnp.dot(a, b, preferred_element_type=F32)


def _dot_nt(a, b):
    return lax.dot_general(a, b, (((1,), (1,)), ((), ())), preferred_element_type=F32)


def _dot_tn(a, b):
    return lax.dot_general(a, b, (((0,), (0,)), ((), ())), preferred_element_type=F32)


def _layer_norm(x, g, b):
    mu = jnp.mean(x, -1, keepdims=True)
    xc = x - mu
    var = jnp.mean(xc * xc, -1, keepdims=True)
    return xc * lax.rsqrt(var + LN_EPS) * g + b


def _adaln_kernel(s_ref, w_ref, b_ref, o_ref):
    s = s_ref[...]
    s = s * jax.nn.sigmoid(s)
    o_ref[0] = jnp.dot(s, w_ref[0], preferred_element_type=F32, precision=HI) + b_ref[0]


def _adaln(c, c_ctx, ada_w, ada_b):
    bsz = c.shape[0]
    rows = 2 * SUBLANE
    s = jnp.zeros((rows, D_MODEL), F32).at[:bsz].set(c).at[bsz].set(c_ctx)
    n_out = N_MOD * D_MODEL
    tn = D_MODEL
    out = pl.pallas_call(
        _adaln_kernel,
        grid=(DEPTH, n_out // tn),
        in_specs=[
            pl.BlockSpec((rows, D_MODEL), lambda i, j: (0, 0)),
            pl.BlockSpec((1, D_MODEL, tn), lambda i, j: (i, 0, j)),
            pl.BlockSpec((1, 1, tn), lambda i, j: (i, 0, j)),
        ],
        out_specs=pl.BlockSpec((1, rows, tn), lambda i, j: (i, 0, j)),
        out_shape=jax.ShapeDtypeStruct((DEPTH, rows, n_out), F32),
        compiler_params=_cparams("arbitrary", "arbitrary"),
        name="adaln",
    )(s, ada_w, ada_b.reshape(DEPTH, 1, n_out))
    return out.reshape(DEPTH, rows, N_MOD, D_MODEL)


def _token_call(body, lat_ins, ctx_ins, shared, out_widths, out_dtypes, *, mod, with_ctx, name,
                scratch_fn=None, n_halo=0):
    bsz, n_lat, _ = lat_ins[0].shape
    n_ctx = ctx_ins[0].shape[1] if with_ctx else 0
    tm = TOKEN_ROWS
    nlb = n_lat // tm
    n_in, n_sh, n_out = len(lat_ins), len(shared), len(out_widths)
    sub_per_blk = tm // SUBLANE

    def lat_map(b, j):
        return (b, jnp.minimum(j, nlb - 1), 0)

    def ctx_map(b, j):
        return (b, 0, 0)

    def prev_map(b, j):
        return (b, jnp.maximum(jnp.minimum(j, nlb - 1) * sub_per_blk - 1, 0), 0)

    def next_map(b, j):
        return (b, jnp.minimum((jnp.minimum(j, nlb - 1) + 1) * sub_per_blk, n_lat // SUBLANE - 1), 0)

    in_specs = [pl.BlockSpec((1, tm, a.shape[2]), lat_map) for a in lat_ins]
    args = list(lat_ins)
    for a in lat_ins[:n_halo]:
        in_specs += [pl.BlockSpec((1, SUBLANE, a.shape[2]), prev_map),
                     pl.BlockSpec((1, SUBLANE, a.shape[2]), next_map)]
        args += [a, a]
    if with_ctx:
        in_specs += [pl.BlockSpec((1, n_ctx, a.shape[2]), ctx_map) for a in ctx_ins]
        args += list(ctx_ins)
    n_mod = 0
    if mod is not None:
        in_specs.append(pl.BlockSpec((1, N_MOD, D_MODEL), lambda b, j: (b, 0, 0)))
        args.append(mod)
        n_mod = 1
        if with_ctx:
            in_specs.append(pl.BlockSpec((1, N_MOD, D_MODEL), lambda b, j: (bsz, 0, 0)))
            args.append(mod)
            n_mod = 2
    in_specs += [_full_spec(s) for s in shared]
    args += list(shared)

    out_specs = [pl.BlockSpec((1, tm, w), lat_map) for w in out_widths]
    out_shape = [jax.ShapeDtypeStruct((bsz, n_lat, w), dt) for w, dt in zip(out_widths, out_dtypes)]
    if with_ctx:
        out_specs += [pl.BlockSpec((1, n_ctx, w), ctx_map) for w in out_widths]
        out_shape += [jax.ShapeDtypeStruct((bsz, n_ctx, w), dt) for w, dt in zip(out_widths, out_dtypes)]
    scratch = scratch_fn(tm) if scratch_fn is not None else []

    def kern(*refs):
        pos = 0
        lat_r = refs[pos:pos + n_in]; pos += n_in
        halo_r = refs[pos:pos + 2 * n_halo]; pos += 2 * n_halo
        ctx_r = ()
        if with_ctx:
            ctx_r = refs[pos:pos + n_in]; pos += n_in
        mod_r = refs[pos:pos + n_mod]; pos += n_mod
        sh_r = refs[pos:pos + n_sh]; pos += n_sh
        lat_o = refs[pos:pos + n_out]; pos += n_out
        ctx_o = ()
        if with_ctx:
            ctx_o = refs[pos:pos + n_out]; pos += n_out
        scr = refs[pos:]
        j = pl.program_id(1)

        def run(ins, modr, outs, halo):
            extra = (halo,) if n_halo else ()
            vals = body(ins, modr, sh_r, scr, *extra)
            for o, v in zip(outs, vals):
                o[0] = v.astype(o.dtype)

        lat_halo = (halo_r[0::2], halo_r[1::2], j > 0, j < nlb - 1) if n_halo else None
        if with_ctx:

            @pl.when(j < nlb)
            def _():
                run(lat_r, mod_r[0] if n_mod else None, lat_o, lat_halo)

            @pl.when(j == nlb)
            def _():
                run(ctx_r, mod_r[1] if n_mod else None, ctx_o, None)
        else:
            run(lat_r, mod_r[0] if n_mod else None, lat_o, lat_halo)

    outs = pl.pallas_call(
        kern,
        grid=(bsz, nlb + (1 if with_ctx else 0)),
        in_specs=in_specs,
        out_specs=out_specs,
        out_shape=out_shape,
        scratch_shapes=scratch,
        compiler_params=_cparams("arbitrary", "arbitrary"),
        name=name,
    )(*args)
    lat_out = list(outs[:n_out])
    ctx_out = list(outs[n_out:]) if with_ctx else [None] * n_out
    return lat_out, ctx_out


def _mod_rows(mod_ref, i):
    return mod_ref[0, i:i + 1, :], mod_ref[0, i + 1:i + 2, :], mod_ref[0, i + 2:i + 3, :]


def _ffn_body(ins, mod_ref, sh, scr, *, si):
    h = ins[0][0]
    w_in, w_out, g, b = sh
    act = scr[0]
    shift, scale, gate = _mod_rows(mod_ref, si)
    rows = h.shape[0]
    xb = (h * (1.0 + scale) + shift).astype(BF16)
    for j in range(D_FF // FF_CHUNK):
        lo = j * FF_CHUNK
        a = _dot(xb, w_in[:, lo:lo + FF_CHUNK])
        bb = _dot(xb, w_in[:, D_FF + lo:D_FF + lo + FF_CHUNK])
        act[0:rows, lo:lo + FF_CHUNK] = (a * jax.nn.sigmoid(a) * bb).astype(BF16)
    y = _dot(act[0:rows, :], w_out[...])
    return [_layer_norm(ALPHA * h + gate * (MACARON * y), g[...], b[...])]


def _ffn(h_lat, h_ctx, mod, si, w_in, w_out, g, b, with_ctx=True):
    body = functools.partial(_ffn_body, si=si)
    lat, ctx = _token_call(
        body, [h_lat], [h_ctx], [w_in, w_out, g.reshape(1, -1), b.reshape(1, -1)],
        [D_MODEL], [F32], mod=mod, with_ctx=with_ctx, name="ffn_half_step",
        scratch_fn=lambda tm: [pltpu.VMEM((tm, D_FF), BF16)])
    return lat[0], ctx[0]


def _proj_body(ins, mod_ref, sh, scr, *, si):
    h = ins[0][0]
    shift, scale, _ = _mod_rows(mod_ref, si)
    xb = (h * (1.0 + scale) + shift).astype(BF16)
    return [_dot(xb, w[...]) for w in sh]


def _proj(h_lat, h_ctx, mod, si, ws):
    body = functools.partial(_proj_body, si=si)
    return _token_call(body, [h_lat], [h_ctx], ws, [w.shape[1] for w in ws], [F32] * len(ws),
                       mod=mod, with_ctx=True, name="mixer_in_proj")


def _outproj_body(ins, mod_ref, sh, scr, *, gi):
    h = ins[0][0]
    n_m = len(ins) - 1
    ws, g, b = sh[:n_m], sh[n_m], sh[n_m + 1]
    gate = mod_ref[0, gi:gi + 1, :]
    y = _dot(ins[1][0], ws[0][...])
    for k in range(1, n_m):
        y = y + _dot(ins[1 + k][0], ws[k][...])
    return [_layer_norm(ALPHA * h + gate * y, g[...], b[...])]


def _outproj_norm(h_lat, h_ctx, ms_lat, ms_ctx, ws, mod, gi, g, b, with_ctx=True):
    body = functools.partial(_outproj_body, gi=gi)
    lat, ctx = _token_call(body, [h_lat] + ms_lat, [h_ctx] + ms_ctx, list(ws) + [g.reshape(1, -1), b.reshape(1, -1)],
                           [D_MODEL], [F32], mod=mod, with_ctx=with_ctx, name="mixer_out_proj_norm")
    return lat[0], ctx[0]


def _chunk_call(body, pairs, tables, shared, out_widths, out_dtypes, *, chunk, rev, scratch, name):
    bsz, n_lat = pairs[0][0].shape[:2]
    n_ctx = pairs[0][1].shape[1]
    ncl, ncc = n_lat // chunk, n_ctx // chunk

    def lat_idx(n):
        i = (ncl - 1 - (n - ncc)) if rev else (n - ncc)
        return jnp.clip(i, 0, ncl - 1)

    def ctx_idx(n):
        i = (ncc - 1 - n) if rev else n
        return jnp.clip(i, 0, ncc - 1)

    in_specs, args = [], []
    for lat, ctx, w, cb in pairs:
        in_specs.append(pl.BlockSpec((1, chunk, w), lambda b, n, cb=cb: (b, lat_idx(n), cb)))
        in_specs.append(pl.BlockSpec((1, chunk, w), lambda b, n, cb=cb: (b, ctx_idx(n), cb)))
        args += [lat, ctx]
    for t in tables:
        in_specs.append(pl.BlockSpec((chunk, t.shape[1]), lambda b, n: (lat_idx(n), 0)))
        args.append(t)
    in_specs += [_full_spec(s) for s in shared]
    args += list(shared)
    out_specs, out_shape = [], []
    for w, dt in zip(out_widths, out_dtypes):
        out_specs.append(pl.BlockSpec((1, chunk, w), lambda b, n: (b, lat_idx(n), 0)))
        out_specs.append(pl.BlockSpec((1, chunk, w), lambda b, n: (b, ctx_idx(n), 0)))
        out_shape.append(jax.ShapeDtypeStruct((bsz, n_lat, w), dt))
        out_shape.append(jax.ShapeDtypeStruct((bsz, n_ctx, w), dt))
    n_p, n_t, n_s, n_o = len(pairs), len(tables), len(shared), len(out_widths)

    def kern(*refs):
        pos = 0
        pr = refs[pos:pos + 2 * n_p]; pos += 2 * n_p
        tr = refs[pos:pos + n_t]; pos += n_t
        sh = refs[pos:pos + n_s]; pos += n_s
        orf = refs[pos:pos + 2 * n_o]; pos += 2 * n_o
        scr = refs[pos:]
        n = pl.program_id(1)
        is_ctx = n < ncc
        vals = [jnp.where(is_ctx, pr[2 * i + 1][0], pr[2 * i][0]) for i in range(n_p)]
        outs = body(n, is_ctx, vals, [t[...] for t in tr], sh, scr)
        for i, v in enumerate(outs):
            lat_o, ctx_o = orf[2 * i], orf[2 * i + 1]

            @pl.when(is_ctx)
            def _(ctx_o=ctx_o, v=v):
                ctx_o[0] = v.astype(ctx_o.dtype)

            @pl.when(jnp.logical_not(is_ctx))
            def _(lat_o=lat_o, v=v):
                lat_o[0] = v.astype(lat_o.dtype)

    outs = pl.pallas_call(
        kern,
        grid=(bsz, ncl + ncc),
        in_specs=in_specs,
        out_specs=out_specs,
        out_shape=out_shape,
        scratch_shapes=scratch,
        compiler_params=_cparams("arbitrary", "arbitrary"),
        name=name,
    )(*args)
    return [(outs[2 * i], outs[2 * i + 1]) for i in range(n_o)]


S5_WIDTH = 256
S5_GROUP = 16
S5_GROUPS = 16
S5_STATE = 64
S5_T = 4
S5_ROW = S5_T * S5_WIDTH
S5_NSTATE = S5_GROUPS * S5_STATE


def _s5_operators(lam_re, lam_im, log_dt, b_re, b_im, c_re, c_im):
    t_len, g_n, p_n, c_n = S5_T, S5_GROUPS, S5_STATE, S5_GROUP
    dt = jnp.exp(log_dt)[..., None]
    a_r, a_i = lam_re * dt, lam_im * dt
    mag = jnp.exp(a_r)
    lb_re, lb_im = mag * jnp.cos(a_i), mag * jnp.sin(a_i)
    den = lam_re * lam_re + lam_im * lam_im
    nr = lb_re - 1.0
    coef_re = (nr * lam_re + lb_im * lam_im) / den
    coef_im = (lb_im * lam_re - nr * lam_im) / den
    bb_re = coef_re[..., None] * b_re - coef_im[..., None] * b_im
    bb_im = coef_re[..., None] * b_im + coef_im[..., None] * b_re

    def lam_pow(e):
        m = jnp.exp(a_r * e)
        return m * jnp.cos(a_i * e), m * jnp.sin(a_i * e)

    eye = jnp.eye(g_n, dtype=F32)

    def kern_tau(d, tau):
        pr, pi = lam_pow(float(tau))
        cp_re = c_re * pr[d][:, None, :] - c_im * pi[d][:, None, :]
        cp_im = c_re * pi[d][:, None, :] + c_im * pr[d][:, None, :]
        k = (jnp.einsum("gcp,gpe->gec", cp_re, bb_re[d], precision=HI)
             - jnp.einsum("gcp,gpe->gec", cp_im, bb_im[d], precision=HI))
        return k

    k0 = [kern_tau(0, tau) for tau in range(t_len)]
    k1 = [kern_tau(1, tau) for tau in range(t_len)]
    zero = jnp.zeros_like(k0[0])
    rows = []
    for s in range(t_len):
        cols = []
        for t in range(t_len):
            blk = zero
            if s <= t:
                blk = blk + k0[t - s]
            if s >= t:
                blk = blk + k1[s - t]
            cols.append(blk)
        rows.append(jnp.stack(cols, 0))
    kst = jnp.stack(rows, 0)
    m_op = jnp.einsum("stgec,gh->sgethc", kst, eye).reshape(S5_ROW, S5_ROW)

    w_ops, v_ops, lam_t = [], [], []
    for d in range(2):
        w_re, w_im, v_re, v_im = [], [], [], []
        for s in range(t_len):
            pr, pi = lam_pow(float(t_len - 1 - s) if d == 0 else float(s))
            w_re.append(pr[d][..., None] * bb_re[d] - pi[d][..., None] * bb_im[d])
            w_im.append(pr[d][..., None] * bb_im[d] + pi[d][..., None] * bb_re[d])
        for t in range(t_len):
            pr, pi = lam_pow(float(t + 1) if d == 0 else float(t_len - t))
            v_re.append(c_re * pr[d][:, None, :] - c_im * pi[d][:, None, :])
            v_im.append(-(c_re * pi[d][:, None, :] + c_im * pr[d][:, None, :]))
        for w in (w_re, w_im):
            w_ops.append(jnp.einsum("sgpe,gh->sgehp", jnp.stack(w, 0), eye).reshape(S5_ROW, S5_NSTATE))
        for v in (v_re, v_im):
            v_ops.append(jnp.einsum("tgcp,gh->gpthc", jnp.stack(v, 0), eye).reshape(S5_NSTATE, S5_ROW))
        pr, pi = lam_pow(float(t_len))
        lam_t.append((pr[d].reshape(1, S5_NSTATE), pi[d].reshape(1, S5_NSTATE)))
    w_cat = jnp.concatenate([m_op] + w_ops, axis=1).astype(BF16)
    return w_cat, [v.astype(BF16) for v in v_ops], lam_t


def _s5_drive_body(ins, mod_ref, sh, scr):
    xb = ins[0][0].astype(BF16)
    w = sh[0]
    n = w.shape[1] // S5_ROW
    return [_dot(xb, w[:, i * S5_ROW:(i + 1) * S5_ROW]) for i in range(n)]


def _s5_state_scan(b_re, b_im, lam, rev):
    (bre_l, bre_c), (bim_l, bim_c) = b_re, b_im
    bsz, nl, w = bre_l.shape
    nc = bre_c.shape[1]

    def flat(a):
        return a.reshape(bsz * a.shape[1], w)

    def kern(brl, bil, brc, bic, lr_ref, li_ref, hrl, hil, hrc, hic):
        lr = jnp.broadcast_to(lr_ref[...], (bsz, LANE))
        li = jnp.broadcast_to(li_ref[...], (bsz, LANE))

        def make(br, bi, hr_o, hi_o, nrows):
            def step(i, carry):
                hr, hi = carry
                n = (nrows - 1 - i) if rev else i
                idx = pl.ds(n, bsz, stride=nrows)
                hr_o[idx, :] = hr
                hi_o[idx, :] = hi
                xr, xi = br[idx, :], bi[idx, :]
                return lr * hr - li * hi + xr, lr * hi + li * hr + xi
            return step

        z = jnp.zeros((bsz, LANE), F32)
        carry = lax.fori_loop(0, nc, make(brc, bic, hrc, hic, nc), (z, z), unroll=8)
        lax.fori_loop(0, nl, make(brl, bil, hrl, hil, nl), carry, unroll=8)

    def spec(rows):
        return pl.BlockSpec((rows, LANE), lambda j: (0, j))

    lam_spec = pl.BlockSpec((1, LANE), lambda j: (0, j))
    outs = pl.pallas_call(
        kern,
        grid=(w // LANE,),
        in_specs=[spec(bsz * nl), spec(bsz * nl), spec(bsz * nc), spec(bsz * nc), lam_spec, lam_spec],
        out_specs=[spec(bsz * nl), spec(bsz * nl), spec(bsz * nc), spec(bsz * nc)],
        out_shape=[jax.ShapeDtypeStruct((bsz * nl, w), F32), jax.ShapeDtypeStruct((bsz * nl, w), F32),
                   jax.ShapeDtypeStruct((bsz * nc, w), F32), jax.ShapeDtypeStruct((bsz * nc, w), F32)],
        compiler_params=_cparams("arbitrary"),
        name="s5_state_scan",
    )(flat(bre_l), flat(bim_l), flat(bre_c), flat(bim_c), lam[0], lam[1])
    hrl, hil, hrc, hic = outs
    return (hrl.reshape(bsz, nl, w), hrc.reshape(bsz, nc, w)), (hil.reshape(bsz, nl, w), hic.reshape(bsz, nc, w))


def _s5_finish_body(ins, mod_ref, sh, scr):
    yi, h0r, h0i, h1r, h1i, u = [r[0] for r in ins]
    v0r, v0i, v1r, v1i, d_skip, glu_w, glu_b = sh
    y = yi + d_skip[...] * u
    for h, v in ((h0r, v0r), (h0i, v0i), (h1r, v1r), (h1i, v1i)):
        y = y + _dot(h.astype(BF16), v[...])
    z = jax.nn.gelu(y)
    return [z * jax.nn.sigmoid(_dot(z.astype(BF16), glu_w[...]) + glu_b[...])]


def _s5_branch(u_lat, u_ctx, ops, d_skip, glu_w, glu_b):
    w_cat, v_ops, lam_t = ops
    bsz, n_lat, _ = u_lat.shape
    n_ctx = u_ctx.shape[1]
    u4l = u_lat.reshape(bsz, n_lat // S5_T, S5_ROW)
    u4c = u_ctx.reshape(bsz, n_ctx // S5_T, S5_ROW)
    lat, ctx = _token_call(_s5_drive_body, [u4l], [u4c], [w_cat], [S5_ROW] * 5, [F32] * 5,
                           mod=None, with_ctx=True, name="s5_drive")
    pairs = list(zip(lat, ctx))
    h0 = _s5_state_scan(pairs[1], pairs[2], lam_t[0], rev=False)
    h1 = _s5_state_scan(pairs[3], pairs[4], lam_t[1], rev=True)
    eye_t = jnp.eye(S5_T, dtype=F32)
    glu_k = jnp.kron(eye_t, glu_w).astype(BF16)
    fin_l = [pairs[0][0], h0[0][0], h0[1][0], h1[0][0], h1[1][0], u4l]
    fin_c = [pairs[0][1], h0[0][1], h0[1][1], h1[0][1], h1[1][1], u4c]
    shared = list(v_ops) + [jnp.tile(d_skip, S5_T).reshape(1, S5_ROW), glu_k, jnp.tile(glu_b, S5_T).reshape(1, S5_ROW)]
    lat, ctx = _token_call(_s5_finish_body, fin_l, fin_c, shared, [S5_ROW], [BF16],
                           mod=None, with_ctx=True, name="s5_finish")
    return lat[0].reshape(bsz, n_lat, S5_WIDTH), ctx[0].reshape(bsz, n_ctx, S5_WIDTH)


RET_HEADS = 6
RET_HEAD_DIM = 128
RET_WIDTH = RET_HEADS * RET_HEAD_DIM
RET_CHUNK = 256
ROPE_BASE = 10000.0


def _rope_tables(n_lat):
    nf = RET_HEAD_DIM // 4
    rows = n_lat // GRID_W
    freqs = ROPE_BASE ** (-jnp.arange(nf, dtype=F32) / nf)
    pr = jnp.broadcast_to(jnp.arange(rows, dtype=F32)[:, None], (rows, GRID_W)).reshape(-1)
    pc = jnp.broadcast_to(jnp.arange(GRID_W, dtype=F32)[None, :], (rows, GRID_W)).reshape(-1)
    ang = jnp.concatenate([pr[:, None] * freqs, pc[:, None] * freqs], -1)
    cos, sin = jnp.cos(ang), jnp.sin(ang)
    return jnp.concatenate([cos, cos], -1), jnp.concatenate([-sin, sin], -1)


def _ret_tables(log_rate):
    c = RET_CHUNK
    lg = -jnp.exp(log_rate.astype(F32))
    idx = jnp.arange(c, dtype=F32)
    diff = idx[:, None] - idx[None, :]
    past, fut = diff >= 0, diff <= 0
    d0 = jnp.where(past, jnp.exp(jnp.where(past, diff, 0.0)[None] * lg[0][:, None, None]), 0.0)
    d1 = jnp.where(fut, jnp.exp(jnp.where(fut, -diff, 0.0)[None] * lg[1][:, None, None]), 0.0)

    def rep(t):
        return jnp.repeat(t, RET_HEAD_DIM, axis=-1)

    fwd = (rep(jnp.exp((idx + 1.0)[:, None] * lg[0])), rep(jnp.exp((c - 1.0 - idx)[:, None] * lg[0])),
           rep(jnp.exp(c * lg[0])[None]))
    bwd = (rep(jnp.exp((c - idx)[:, None] * lg[1])), rep(jnp.exp(idx[:, None] * lg[1])),
           rep(jnp.exp(c * lg[1])[None]))
    return d0 + d1, fwd, bwd


def _rope(x, cos, sin):
    return x * cos + pltpu.roll(x, RET_HEAD_DIM // 2, 1) * sin


def _ret_heads(is_ctx, q, k, cos, sin, h):
    sl = slice(h * RET_HEAD_DIM, (h + 1) * RET_HEAD_DIM)
    qh, kh = q[:, sl], k[:, sl]
    qh = jnp.where(is_ctx, qh, _rope(qh, cos, sin))
    kh = jnp.where(is_ctx, kh, _rope(kh, cos, sin)) * (RET_HEAD_DIM ** -0.5)
    return sl, qh, kh


def _ret_bwd_body(n, is_ctx, vals, tabs, sh, scr):
    q, k, v = vals
    cos, sin = tabs
    xi, zeta, gc = sh
    state = scr[0]

    @pl.when(n == 0)
    def _():
        state[...] = jnp.zeros_like(state)

    outs = []
    for h in range(RET_HEADS):
        sl, qh, kh = _ret_heads(is_ctx, q, k, cos, sin, h)
        s_old = state[h]
        outs.append(_dot((qh * xi[:, sl]).astype(BF16), s_old.astype(BF16)))
        state[h] = gc[:, sl] * s_old + _dot_tn((kh * zeta[:, sl]).astype(BF16), v[:, sl].astype(BF16))
    return [jnp.concatenate(outs, -1)]


def _ret_fwd_body(n, is_ctx, vals, tabs, sh, scr):
    q, k, v, g, o_bwd = vals
    cos, sin = tabs
    dcomb, xi, zeta, gc = sh
    state = scr[0]

    @pl.when(n == 0)
    def _():
        state[...] = jnp.zeros_like(state)

    outs = []
    for h in range(RET_HEADS):
        sl, qh, kh = _ret_heads(is_ctx, q, k, cos, sin, h)
        vb = v[:, sl].astype(BF16)
        s_old = state[h]
        scores = _dot_nt(qh.astype(BF16), kh.astype(BF16)) * dcomb[h]
        o = (_dot(scores.astype(BF16), vb) + _dot((qh * xi[:, sl]).astype(BF16), s_old.astype(BF16))
             + o_bwd[:, sl])
        state[h] = gc[:, sl] * s_old + _dot_tn((kh * zeta[:, sl]).astype(BF16), vb)
        mu = jnp.mean(o, -1, keepdims=True)
        oc = o - mu
        var = jnp.mean(oc * oc, -1, keepdims=True)
        gh = g[:, sl]
        outs.append(gh * jax.nn.sigmoid(gh) * (oc * lax.rsqrt(var + LN_EPS)))
    return [jnp.concatenate(outs, -1)]


def _retention_branch(p_lat, p_ctx, log_rate):
    n_lat = p_lat.shape[1]
    cos, sin = _rope_tables(n_lat)
    dcomb, fwd, bwd = _ret_tables(log_rate)
    w = RET_WIDTH
    state = [pltpu.VMEM((RET_HEADS, RET_HEAD_DIM, RET_HEAD_DIM), F32)]
    qkv = [(p_lat, p_ctx, w, i) for i in range(3)]
    (o_bwd,) = _chunk_call(_ret_bwd_body, qkv, [cos, sin], list(bwd), [w], [F32],
                           chunk=RET_CHUNK, rev=True, scratch=state, name="retention_bwd")
    pairs = qkv + [(p_lat, p_ctx, w, 3), (o_bwd[0], o_bwd[1], w, 0)]
    (r,) = _chunk_call(_ret_fwd_body, pairs, [cos, sin], [dcomb] + list(fwd), [w], [BF16],
                       chunk=RET_CHUNK, rev=False, scratch=state, name="retention_fwd")
    return r


def _mixer_ab(h_lat, h_ctx, mod, w_in, w_out, s5_ops, s5_d, s5_glu_w, s5_glu_b, ret_log_rate):
    w_u = w_in[:, :S5_WIDTH].astype(BF16)
    w_qkvg = w_in[:, S5_WIDTH:].astype(BF16)
    (p_lat, u_lat), (p_ctx, u_ctx) = _proj(h_lat, h_ctx, mod, 3, [w_qkvg, w_u])
    a_lat, a_ctx = _s5_branch(u_lat, u_ctx, s5_ops, s5_d, s5_glu_w, s5_glu_b)
    r_lat, r_ctx = _retention_branch(p_lat, p_ctx, ret_log_rate)
    return [a_lat, r_lat], [a_ctx, r_ctx], [w_out[:S5_WIDTH].astype(BF16), w_out[S5_WIDTH:].astype(BF16)]


def _mm(a, b, exact=False):
    if exact:
        return jnp.dot(a, b, preferred_element_type=F32, precision=HI)
    return _dot(a.astype(BF16), b.astype(BF16))


def _mm_nt(a, b):
    return _dot_nt(a.astype(BF16), b.astype(BF16))


def _mm_tn(a, b):
    return _dot_tn(a.astype(BF16), b.astype(BF16))


def _softplus(x):
    return jnp.maximum(x, 0.0) + jnp.log1p(jnp.exp(-jnp.abs(x)))


def _shift_rows(x, s, prev8, next8):
    rows = x.shape[0]
    rolled = pltpu.roll(x, (-s) % rows, 0)
    r8 = lax.broadcasted_iota(jnp.int32, (SUBLANE, x.shape[1]), 0)
    if s < 0:
        edge = jnp.where(r8 < -s, pltpu.roll(prev8, -s, 0), rolled[:SUBLANE])
        return jnp.concatenate([edge, rolled[SUBLANE:]], 0)
    edge = jnp.where(r8 >= SUBLANE - s, pltpu.roll(next8, SUBLANE - s, 0), rolled[rows - SUBLANE:])
    return jnp.concatenate([rolled[:rows - SUBLANE], edge], 0)


def _halo_rows(halo, i, width):
    if halo is None:
        z = jnp.zeros((SUBLANE, width), F32)
        return z, z
    prev_r, next_r, has_prev, has_next = halo
    return (jnp.where(has_prev, prev_r[i][0], 0.0), jnp.where(has_next, next_r[i][0], 0.0))


def _order_masks(c, rev):
    ri = lax.broadcasted_iota(jnp.int32, (c, c), 0)
    ci = lax.broadcasted_iota(jnp.int32, (c, c), 1)
    if rev:
        return ri <= ci, ri < ci
    return ri >= ci, ri > ci


TRI_BASE = 8


def _tri_inverse(n_mat):
    c = n_mat.shape[0]
    ri = lax.broadcasted_iota(jnp.int32, (c, c), 0)
    ci = lax.broadcasted_iota(jnp.int32, (c, c), 1)
    def same_block(bits):
        return jnp.right_shift(ri, bits) == jnp.right_shift(ci, bits)

    bits = TRI_BASE.bit_length() - 1
    same = same_block(bits)
    p = jnp.where(same, n_mat, 0.0)
    x = jnp.where(ri == ci, 1.0, 0.0) + p
    k = 2
    while k < TRI_BASE:
        p = _mm(p, p)
        x = x + _mm(x, p)
        k *= 2
    while (1 << bits) < c:
        bits += 1
        same2 = same_block(bits)
        off = jnp.where(jnp.logical_and(same2, jnp.logical_not(same)), n_mat, 0.0)
        x = x + _mm(_mm(x, off), x)
        same = same2
    return x


def _seg_sum(x, seg):
    hi = x.astype(BF16)
    lo = (x - hi.astype(F32)).astype(BF16)
    return _dot(hi, seg) + _dot(lo, seg)


GDN_HEADS = 4
GDN_HEAD_DIM = 128
GDN_WIDTH = GDN_HEADS * GDN_HEAD_DIM
GDN_CONV = 5
GDN_CHUNK = 64


def _gdn_prep_body(ins, mod_ref, sh, scr, halo):
    qkv, sm = ins[0][0], ins[1][0]
    conv_w, a_neg, dt_bias = sh
    prev8, next8 = _halo_rows(halo, 0, qkv.shape[1])
    half = GDN_CONV // 2
    acc = conv_w[half:half + 1, :] * qkv
    for i in range(GDN_CONV):
        if i != half:
            acc = acc + conv_w[i:i + 1, :] * _shift_rows(qkv, i - half, prev8, next8)
    y = acc * jax.nn.sigmoid(acc)
    outs = []
    for part, scale in ((0, GDN_HEAD_DIM ** -0.5), (1, 1.0)):
        cols = []
        for h in range(GDN_HEADS):
            lo = part * GDN_WIDTH + h * GDN_HEAD_DIM
            t = y[:, lo:lo + GDN_HEAD_DIM]
            cols.append(t * (lax.rsqrt(jnp.sum(t * t, -1, keepdims=True) + 1e-6) * scale))
        outs.append(jnp.concatenate(cols, -1))
    outs.append(y[:, 2 * GDN_WIDTH:])
    lane = lax.broadcasted_iota(jnp.int32, sm.shape, 1)
    g = a_neg[...] * _softplus(sm + dt_bias[...])
    outs.append(jnp.where(lane < 2 * GDN_HEADS, g, jax.nn.sigmoid(sm)))
    return outs


def _gdn_body(n, is_ctx, vals, tabs, sh, scr, *, rev, last):
    q, k, v, sm = vals[:4]
    state = scr[0]
    c = GDN_CHUNK

    @pl.when(n == 0)
    def _():
        state[...] = jnp.zeros_like(state)

    later, strict = _order_masks(c, rev)
    gcum = _mm(later.astype(F32), sm, exact=True)
    gcum_t = gcum.T
    end = 0 if rev else c - 1
    d = 1 if rev else 0
    outs = []
    for h in range(GDN_HEADS):
        sl = slice(h * GDN_HEAD_DIM, (h + 1) * GDN_HEAD_DIM)
        col = d * GDN_HEADS + h
        bcol = 2 * GDN_HEADS + col
        gc, gr = gcum[:, col:col + 1], gcum_t[col:col + 1, :]
        beta = sm[:, bcol:bcol + 1]
        dec = jnp.where(later, jnp.exp(jnp.where(later, gc - gr, 0.0)), 0.0)
        qh, kh, vh = q[:, sl], k[:, sl], v[:, sl]
        kb = kh * beta
        a_mat = jnp.where(strict, _mm_nt(kb, kh) * dec, 0.0)
        t_mat = _tri_inverse(-a_mat)
        eg = jnp.exp(gc)
        w = _mm(t_mat, kb * eg)
        u = _mm(t_mat, vh * beta)
        attn = _mm_nt(qh, kh) * dec
        g_end = gc[end:end + 1, :]
        s_old = state[h]
        v_new = u - _mm(w, s_old)
        outs.append(_mm(qh * eg, s_old) + _mm(attn, v_new))
        state[h] = s_old * jnp.exp(g_end) + _mm_tn(kh * jnp.exp(g_end - gc), v_new)
    o = jnp.concatenate(outs, -1)
    if not last:
        return [o]
    z, o_prev = vals[4], vals[5]
    norm_w = sh[0]
    o = o + o_prev
    cols = []
    for h in range(GDN_HEADS):
        sl = slice(h * GDN_HEAD_DIM, (h + 1) * GDN_HEAD_DIM)
        t = o[:, sl]
        cols.append(t * lax.rsqrt(jnp.mean(t * t, -1, keepdims=True) + 1e-6) * norm_w[...])
    return [jnp.concatenate(cols, -1) * (z * jax.nn.sigmoid(z))]


def _deltanet_branch(qkv, z, sm, conv_w, a_log, dt_bias, norm_w):
    pad = sm[0].shape[2] - 2 * GDN_HEADS
    a_neg = jnp.pad(-jnp.exp(a_log.astype(F32)).reshape(1, -1), ((0, 0), (0, pad)))
    dtb = jnp.pad(dt_bias.astype(F32).reshape(1, -1), ((0, 0), (0, pad)))
    w = GDN_WIDTH
    lat, ctx = _token_call(_gdn_prep_body, [qkv[0], sm[0]], [qkv[1], sm[1]], [conv_w, a_neg, dtb],
                           [w, w, w, sm[0].shape[2]], [F32] * 4, mod=None, with_ctx=True,
                           name="gdn_prep", n_halo=1)
    pairs = [(lat[i], ctx[i], lat[i].shape[2], 0) for i in range(4)]
    state = [pltpu.VMEM((GDN_HEADS, GDN_HEAD_DIM, GDN_HEAD_DIM), F32)]
    (o_bwd,) = _chunk_call(functools.partial(_gdn_body, rev=True, last=False), pairs, [], [], [w], [F32],
                           chunk=GDN_CHUNK, rev=True, scratch=state, name="gdn_bwd")
    pairs = pairs + [(z[0], z[1], w, 0), (o_bwd[0], o_bwd[1], w, 0)]
    (out,) = _chunk_call(functools.partial(_gdn_body, rev=False, last=True), pairs, [], [norm_w.reshape(1, -1)],
                         [w], [BF16], chunk=GDN_CHUNK, rev=False, scratch=state, name="gdn_fwd")
    return out


RWKV_HEADS = 8
RWKV_HEAD_DIM = 64
RWKV_WIDTH = RWKV_HEADS * RWKV_HEAD_DIM
RWKV_DECAY_LORA = 32
RWKV_AAA_LORA = 32
RWKV_GATE_LORA = 96
RWKV_GN_EPS = 64e-5
RWKV_CHUNK = 64
RWKV_PAIRS = RWKV_WIDTH // LANE


def _rwkv_prep_body(ins, mod_ref, sh, scr, halo):
    rkv, sm = ins[0][0], ins[1][0]
    mu_main, mu_sm, k_k, k_a, r_k, w0, a0, lora_w, seg = sh
    w = RWKV_WIDTH

    def lerp(x, mu, i):
        prev8, next8 = _halo_rows(halo, i, x.shape[1])
        xs = 0.5 * (_shift_rows(x, -1, prev8, next8) + _shift_rows(x, 1, prev8, next8))
        return x + mu[...] * (xs - x)

    rkv = lerp(rkv, mu_main, 0)
    sm = lerp(sm, mu_sm, 1)
    r, k, v = rkv[:, :w], rkv[:, w:2 * w], rkv[:, 2 * w:]
    lane = lax.broadcasted_iota(jnp.int32, sm.shape, 1)
    n_dec, n_aaa = 2 * RWKV_DECAY_LORA, 2 * RWKV_AAA_LORA
    t = jnp.where(lane < n_dec, jnp.tanh(sm), jnp.where(lane < n_dec + n_aaa, sm, jax.nn.sigmoid(sm)))
    lo = _dot(t.astype(BF16), lora_w[...])
    kk = k * k_k[...]
    kk = kk * lax.rsqrt(_seg_sum(kk * kk, seg[...]) + 1e-6)
    outs = [r, v, kk]
    k_sum = None
    for d in range(2):
        w_log = -_softplus(-(w0[d:d + 1, :] + lo[:, d * w:(d + 1) * w])) - 0.5
        a = jax.nn.sigmoid(a0[d:d + 1, :] + lo[:, (2 + d) * w:(3 + d) * w])
        k_d = k * (1.0 + (a - 1.0) * k_a[...])
        outs += [k_d, -jnp.exp(w_log), kk * a]
        k_sum = k_d if k_sum is None else k_sum + k_d
    outs.append(lo[:, 4 * w:])
    outs.append(_seg_sum(r * k_sum * r_k[...], seg[...]) * v)
    return outs


def _rwkv_body(n, is_ctx, vals, tabs, sh, scr, *, rev, last):
    r, k, v, kk, lw, b = vals[:6]
    state = scr[0]
    c = RWKV_CHUNK

    @pl.when(n == 0)
    def _():
        state[...] = jnp.zeros_like(state)

    later, strict = _order_masks(c, rev)
    cum = _mm(later.astype(F32), lw, exact=True)
    end = 0 if rev else c - 1
    cum_end = cum[end:end + 1, :]
    e_inv = jnp.exp(-cum)
    e_tail = jnp.exp(cum_end - cum)
    r_t = r * jnp.exp(cum)
    a_t = -kk * jnp.exp(cum - lw)
    b_t, k_t = b * e_inv, k * e_inv
    b_h, k_h = b * e_tail, k * e_tail
    p_end = jnp.exp(cum_end)
    lane = lax.broadcasted_iota(jnp.int32, (c, LANE), 1)
    first = lane < RWKV_HEAD_DIM
    bi = lax.broadcasted_iota(jnp.int32, (LANE, LANE), 0) < RWKV_HEAD_DIM
    bj = lax.broadcasted_iota(jnp.int32, (LANE, LANE), 1) < RWKV_HEAD_DIM
    same_head = bi == bj
    outs = []
    for p in range(RWKV_PAIRS):
        sl = slice(p * LANE, (p + 1) * LANE)
        s_old = state[p]
        v_p = v[:, sl]
        w_t = u_t = None
        rb, rkv_t = [], []
        for m in (first, jnp.logical_not(first)):
            a_x = jnp.where(m, a_t[:, sl], 0.0)
            r_x = jnp.where(m, r_t[:, sl], 0.0)
            a_ab = jnp.where(strict, _mm_nt(a_x, b_t[:, sl]), 0.0)
            a_ak = jnp.where(strict, _mm_nt(a_x, k_t[:, sl]), 0.0)
            rb.append(jnp.where(later, _mm_nt(r_x, b_t[:, sl]), 0.0))
            a_rk = jnp.where(later, _mm_nt(r_x, k_t[:, sl]), 0.0)
            inv = _tri_inverse(a_ab)
            w_x = _mm(inv, a_x)
            u_x = _mm(inv, _mm(a_ak, v_p))
            w_t = w_x if w_t is None else w_t + w_x
            u_t = u_x if u_t is None else jnp.where(first, u_t, u_x)
            rkv_t.append(_mm(a_rk, v_p))
        u = _mm_nt(w_t, s_old) + u_t
        y = _mm_nt(r_t[:, sl], s_old) + jnp.where(first, _mm(rb[0], u) + rkv_t[0], _mm(rb[1], u) + rkv_t[1])
        s_new = s_old * p_end[:, sl] + _mm_tn(u, b_h[:, sl]) + _mm_tn(v_p, k_h[:, sl])
        state[p] = jnp.where(same_head, s_new, 0.0)
        outs.append(y)
    y = jnp.concatenate(outs, -1)
    if not last:
        return [y]
    y_prev, g, bonus = vals[6:9]
    ln_w, ln_b, seg = sh
    y = y + y_prev
    inv_n = 1.0 / RWKV_HEAD_DIM
    mu = _seg_sum(y, seg[...]) * inv_n
    yc = y - mu
    var = _seg_sum(yc * yc, seg[...]) * inv_n
    return [((yc * lax.rsqrt(var + RWKV_GN_EPS)) * ln_w[...] + ln_b[...] + bonus) * g]


def _rwkv7_branch(rkv, sm, mu, w0, w_up, a0, a_up, g_up, k_k, k_a, r_k, ln_w, ln_b):
    w = RWKV_WIDTH
    n_sm = sm[0].shape[2]
    n_used = 2 * RWKV_DECAY_LORA + 2 * RWKV_AAA_LORA + RWKV_GATE_LORA
    mu_main = mu[:3 * w].reshape(1, -1)
    mu_sm = jnp.pad(mu[3 * w:], (0, n_sm - n_used)).reshape(1, -1)
    lora_w = jnp.zeros((n_sm, 5 * w), F32)
    for d in range(2):
        lo = d * RWKV_DECAY_LORA
        lora_w = lora_w.at[lo:lo + RWKV_DECAY_LORA, d * w:(d + 1) * w].set(w_up[d])
        lo = 2 * RWKV_DECAY_LORA + d * RWKV_AAA_LORA
        lora_w = lora_w.at[lo:lo + RWKV_AAA_LORA, (2 + d) * w:(3 + d) * w].set(a_up[d])
    lo = 2 * RWKV_DECAY_LORA + 2 * RWKV_AAA_LORA
    lora_w = lora_w.at[lo:lo + RWKV_GATE_LORA, 4 * w:].set(g_up).astype(BF16)
    head = jnp.arange(w) // RWKV_HEAD_DIM
    seg = (head[:, None] == head[None, :]).astype(BF16)
    row = lambda t: t.reshape(1, -1)
    shared = [mu_main, mu_sm, row(k_k), row(k_a), row(r_k), w0, a0, lora_w, seg]
    lat, ctx = _token_call(_rwkv_prep_body, [rkv[0], sm[0]], [rkv[1], sm[1]], shared,
                           [w] * 11, [F32] * 11, mod=None, with_ctx=True, name="rwkv_prep", n_halo=2)
    pr = [(lat[i], ctx[i], w, 0) for i in range(11)]
    state = [pltpu.VMEM((RWKV_PAIRS, LANE, LANE), F32)]
    (y_bwd,) = _chunk_call(functools.partial(_rwkv_body, rev=True, last=False),
                           [pr[0], pr[6], pr[1], pr[2], pr[7], pr[8]], [], [], [w], [F32],
                           chunk=RWKV_CHUNK, rev=True, scratch=state, name="rwkv_bwd")
    pairs = [pr[0], pr[3], pr[1], pr[2], pr[4], pr[5], (y_bwd[0], y_bwd[1], w, 0), pr[9], pr[10]]
    (out,) = _chunk_call(functools.partial(_rwkv_body, rev=False, last=True), pairs, [],
                         [row(ln_w), row(ln_b), seg], [w], [BF16],
                         chunk=RWKV_CHUNK, rev=False, scratch=state, name="rwkv_fwd")
    return out


def _pad_cols(w, n):
    return jnp.pad(w, ((0, 0), (0, n - w.shape[1])))


def _mixer_cd(h_lat, h_ctx, mod, w_in, w_out, conv_w, a_log, dt_bias, norm_w, mu, w0, w_up, a0, a_up, g_up,
              k_k, k_a, r_k, ln_w, ln_b):
    n_qkv, n_gsm = 3 * GDN_WIDTH, 4 * GDN_HEADS
    n_rkv = 3 * RWKV_WIDTH
    lo_z, lo_g = n_qkv, n_qkv + GDN_WIDTH
    lo_r = lo_g + n_gsm
    lo_s = lo_r + n_rkv
    ws = [w_in[:, :n_qkv], w_in[:, lo_z:lo_g], _pad_cols(w_in[:, lo_g:lo_r], LANE),
          w_in[:, lo_r:lo_s], _pad_cols(w_in[:, lo_s:], 2 * LANE)]
    lat, ctx = _proj(h_lat, h_ctx, mod, 3, [w.astype(BF16) for w in ws])
    pair = lambda i: (lat[i], ctx[i])
    d = _deltanet_branch(pair(0), pair(1), pair(2), conv_w, a_log, dt_bias, norm_w)
    r = _rwkv7_branch(pair(3), pair(4), mu, w0, w_up, a0, a_up, g_up, k_k, k_a, r_k, ln_w, ln_b)
    return [d[0], r[0]], [d[1], r[1]], [w_out[:GDN_WIDTH].astype(BF16), w_out[GDN_WIDTH:].astype(BF16)]


def kernel(x, c, ctx, c_ctx, ada_w, ada_b, ffn_w_in, ffn_w_out, ln_g, ln_b, ab_w_in, ab_w_out, s5_lam_re, s5_lam_im, s5_log_dt, s5_b_re, s5_b_im, s5_c_re, s5_c_im, s5_d, s5_glu_w, s5_glu_b, ret_log_rate, cd_w_in, cd_w_out, gdn_conv_w, gdn_a_log, gdn_dt_bias, gdn_norm_w, rwkv_mu, rwkv_w0, rwkv_w_up, rwkv_a0, rwkv_a_up, rwkv_g_up, rwkv_k_k, rwkv_k_a, rwkv_r_k, rwkv_ln_w, rwkv_ln_b):
    mods = _adaln(c, c_ctx, ada_w, ada_b)
    h_lat, h_ctx = x, ctx
    for i in range(DEPTH):
        mod = mods[i]
        j = i // 2
        keep_ctx = i < DEPTH - 1
        h_lat, h_ctx = _ffn(h_lat, h_ctx, mod, 0, ffn_w_in[i, 0].astype(BF16), ffn_w_out[i, 0].astype(BF16),
                            ln_g[i, 0], ln_b[i, 0])
        if i % 2 == 0:
            s5_ops = _s5_operators(s5_lam_re[j], s5_lam_im[j], s5_log_dt[j], s5_b_re[j], s5_b_im[j],
                                   s5_c_re[j], s5_c_im[j])
            ms_lat, ms_ctx, w_out = _mixer_ab(h_lat, h_ctx, mod, ab_w_in[j], ab_w_out[j], s5_ops, s5_d[j],
                                              s5_glu_w[j], s5_glu_b[j], ret_log_rate[j])
        else:
            ms_lat, ms_ctx, w_out = _mixer_cd(
                h_lat, h_ctx, mod, cd_w_in[j], cd_w_out[j], gdn_conv_w[j], gdn_a_log[j], gdn_dt_bias[j],
                gdn_norm_w[j], rwkv_mu[j], rwkv_w0[j], rwkv_w_up[j], rwkv_a0[j], rwkv_a_up[j], rwkv_g_up[j],
                rwkv_k_k[j], rwkv_k_a[j], rwkv_r_k[j], rwkv_ln_w[j], rwkv_ln_b[j])
        h_lat, h_ctx = _outproj_norm(h_lat, h_ctx, ms_lat, ms_ctx, w_out, mod, 5, ln_g[i, 1], ln_b[i, 1],
                                     with_ctx=keep_ctx)
        h_lat, h_ctx = _ffn(h_lat, h_ctx, mod, 6, ffn_w_in[i, 1].astype(BF16), ffn_w_out[i, 1].astype(BF16),
                            ln_g[i, 2], ln_b[i, 2], with_ctx=keep_ctx)
    return h_lat
```

```python
import functools
import math

import jax
import jax.numpy as jnp
from jax import lax
from jax.experimental import pallas as pl
from jax.experimental.pallas import tpu as pltpu

F32 = jnp.float32
BF16 = jnp.bfloat16
HI = lax.Precision.HIGHEST

D_MODEL = 1024
DEPTH = 4
N_MOD = 9
D_FF = 2816
MACARON = 0.5
ALPHA = (2.0 * DEPTH) ** 0.25
LN_EPS = 1e-5
GRID_W = 64

LANE = 128
SUBLANE = 8
MXU_TILE = 256
VMEM_LIMIT_BYTES = 56 * 1024 * 1024

TOKEN_ROWS = 512
FF_CHUNK = MXU_TILE


def _cparams(*sem):
    return pltpu.CompilerParams(dimension_semantics=sem, vmem_limit_bytes=VMEM_LIMIT_BYTES)


def _full_spec(a):
    nd = a.ndim
    return pl.BlockSpec(a.shape, lambda *_, nd=nd: (0,) * nd)


def _dot(a, b):
    return jnp.dot(a, b, preferred_element_type=F32)


def _dot_nt(a, b):
    return lax.dot_general(a, b, (((1,), (1,)), ((), ())), preferred_element_type=F32)


def _dot_tn(a, b):
    return lax.dot_general(a, b, (((0,), (0,)), ((), ())), preferred_element_type=F32)


def _layer_norm(x, g, b):
    mu = jnp.mean(x, -1, keepdims=True)
    xc = x - mu
    var = jnp.mean(xc * xc, -1, keepdims=True)
    return xc * lax.rsqrt(var + LN_EPS) * g + b


def _adaln_kernel(s_ref, w_ref, b_ref, o_ref):
    s = s_ref[...]
    s = s * jax.nn.sigmoid(s)
    o_ref[0] = jnp.dot(s, w_ref[0], preferred_element_type=F32, precision=HI) + b_ref[0]


def _adaln(c, c_ctx, ada_w, ada_b):
    bsz = c.shape[0]
    rows = 2 * SUBLANE
    s = jnp.zeros((rows, D_MODEL), F32).at[:bsz].set(c).at[bsz].set(c_ctx)
    n_out = N_MOD * D_MODEL
    tn = D_MODEL
    out = pl.pallas_call(
        _adaln_kernel,
        grid=(DEPTH, n_out // tn),
        in_specs=[
            pl.BlockSpec((rows, D_MODEL), lambda i, j: (0, 0)),
            pl.BlockSpec((1, D_MODEL, tn), lambda i, j: (i, 0, j)),
            pl.BlockSpec((1, 1, tn), lambda i, j: (i, 0, j)),
        ],
        out_specs=pl.BlockSpec((1, rows, tn), lambda i, j: (i, 0, j)),
        out_shape=jax.ShapeDtypeStruct((DEPTH, rows, n_out), F32),
        compiler_params=_cparams("arbitrary", "arbitrary"),
        name="adaln",
    )(s, ada_w, ada_b.reshape(DEPTH, 1, n_out))
    return out.reshape(DEPTH, rows, N_MOD, D_MODEL)


def _token_call(body, lat_ins, ctx_ins, shared, out_widths, out_dtypes, *, mod, with_ctx, name,
                scratch_fn=None, n_halo=0, seg=None, joint_out=False):
    bsz = lat_ins[0].shape[0]
    if seg is None:
        seg = (lat_ins[0].shape[1], ctx_ins[0].shape[1] if with_ctx else 0)
    n_lat, n_ctx = seg
    tm = n_ctx if joint_out else TOKEN_ROWS
    nlb = n_lat // tm
    n_in, n_sh, n_out = len(lat_ins), len(shared), len(out_widths)
    sub_per_blk = tm // SUBLANE
    joint_in = [with_ctx and lat_ins[i] is ctx_ins[i] for i in range(n_in)]

    def lat_map(b, j):
        return (b, jnp.minimum(j, nlb - 1), 0)

    def ctx_map(b, j):
        return (b, 0, 0)

    def joint_ctx_map(b, j):
        return (b, n_lat // n_ctx, 0)

    def joint_out_map(b, j):
        return (b, j, 0)

    def prev_map(b, j):
        return (b, jnp.maximum(jnp.minimum(j, nlb - 1) * sub_per_blk - 1, 0), 0)

    def next_map(b, j):
        return (b, jnp.minimum((jnp.minimum(j, nlb - 1) + 1) * sub_per_blk, n_lat // SUBLANE - 1), 0)

    in_specs = [pl.BlockSpec((1, tm, a.shape[2]), lat_map) for a in lat_ins]
    args = list(lat_ins)
    for a in lat_ins[:n_halo]:
        in_specs += [pl.BlockSpec((1, SUBLANE, a.shape[2]), prev_map),
                     pl.BlockSpec((1, SUBLANE, a.shape[2]), next_map)]
        args += [a, a]
    if with_ctx:
        in_specs += [pl.BlockSpec((1, n_ctx, a.shape[2]), joint_ctx_map if jt else ctx_map)
                     for a, jt in zip(ctx_ins, joint_in)]
        args += list(ctx_ins)
    n_mod = 0
    if mod is not None:
        in_specs.append(pl.BlockSpec((1, N_MOD, D_MODEL), lambda b, j: (b, 0, 0)))
        args.append(mod)
        n_mod = 1
        if with_ctx:
            in_specs.append(pl.BlockSpec((1, N_MOD, D_MODEL), lambda b, j: (bsz, 0, 0)))
            args.append(mod)
            n_mod = 2
    in_specs += [_full_spec(s) for s in shared]
    args += list(shared)

    if joint_out:
        out_specs = [pl.BlockSpec((1, tm, w), joint_out_map) for w in out_widths]
        out_shape = [jax.ShapeDtypeStruct((bsz, n_lat + n_ctx, w), dt) for w, dt in zip(out_widths, out_dtypes)]
    else:
        out_specs = [pl.BlockSpec((1, tm, w), lat_map) for w in out_widths]
        out_shape = [jax.ShapeDtypeStruct((bsz, n_lat, w), dt) for w, dt in zip(out_widths, out_dtypes)]
        if with_ctx:
            out_specs += [pl.BlockSpec((1, n_ctx, w), ctx_map) for w in out_widths]
            out_shape += [jax.ShapeDtypeStruct((bsz, n_ctx, w), dt) for w, dt in zip(out_widths, out_dtypes)]
    scratch = scratch_fn(tm) if scratch_fn is not None else []

    def kern(*refs):
        pos = 0
        lat_r = refs[pos:pos + n_in]; pos += n_in
        halo_r = refs[pos:pos + 2 * n_halo]; pos += 2 * n_halo
        ctx_r = ()
        if with_ctx:
            ctx_r = refs[pos:pos + n_in]; pos += n_in
        mod_r = refs[pos:pos + n_mod]; pos += n_mod
        sh_r = refs[pos:pos + n_sh]; pos += n_sh
        lat_o = refs[pos:pos + n_out]; pos += n_out
        ctx_o = lat_o
        if with_ctx and not joint_out:
            ctx_o = refs[pos:pos + n_out]; pos += n_out
        scr = refs[pos:]
        j = pl.program_id(1)

        def run(ins, modr, outs, halo):
            extra = (halo,) if n_halo else ()
            vals = body(ins, modr, sh_r, scr, *extra)
            for o, v in zip(outs, vals):
                o[0] = v.astype(o.dtype)

        lat_halo = (halo_r[0::2], halo_r[1::2], j > 0, j < nlb - 1) if n_halo else None
        if with_ctx:

            @pl.when(j < nlb)
            def _():
                run(lat_r, mod_r[0] if n_mod else None, lat_o, lat_halo)

            @pl.when(j == nlb)
            def _():
                run(ctx_r, mod_r[1] if n_mod else None, ctx_o, None)
        else:
            run(lat_r, mod_r[0] if n_mod else None, lat_o, lat_halo)

    outs = pl.pallas_call(
        kern,
        grid=(bsz, nlb + (1 if with_ctx else 0)),
        in_specs=in_specs,
        out_specs=out_specs,
        out_shape=out_shape,
        scratch_shapes=scratch,
        compiler_params=_cparams("arbitrary", "arbitrary"),
        name=name,
    )(*args)
    if joint_out:
        return list(outs)
    lat_out = list(outs[:n_out])
    ctx_out = list(outs[n_out:]) if with_ctx else [None] * n_out
    return lat_out, ctx_out


def _mod_rows(mod_ref, i):
    return mod_ref[0, i:i + 1, :], mod_ref[0, i + 1:i + 2, :], mod_ref[0, i + 2:i + 3, :]


def _ffn_body(ins, mod_ref, sh, scr, *, si):
    h = ins[0][0]
    w_in, w_out, g, b = sh
    act = scr[0]
    shift, scale, gate = _mod_rows(mod_ref, si)
    rows = h.shape[0]
    xb = (h * (1.0 + scale) + shift).astype(BF16)
    for j in range(D_FF // FF_CHUNK):
        lo = j * FF_CHUNK
        a = _dot(xb, w_in[:, lo:lo + FF_CHUNK])
        bb = _dot(xb, w_in[:, D_FF + lo:D_FF + lo + FF_CHUNK])
        act[0:rows, lo:lo + FF_CHUNK] = (a * jax.nn.sigmoid(a) * bb).astype(BF16)
    y = _dot(act[0:rows, :], w_out[...])
    return [_layer_norm(ALPHA * h + gate * (MACARON * y), g[...], b[...])]


def _ffn(h_lat, h_ctx, mod, si, w_in, w_out, g, b, with_ctx=True):
    body = functools.partial(_ffn_body, si=si)
    lat, ctx = _token_call(
        body, [h_lat], [h_ctx], [w_in, w_out, g.reshape(1, -1), b.reshape(1, -1)],
        [D_MODEL], [F32], mod=mod, with_ctx=with_ctx, name="ffn_half_step",
        scratch_fn=lambda tm: [pltpu.VMEM((tm, D_FF), BF16)])
    return lat[0], ctx[0]


def _proj_body(ins, mod_ref, sh, scr, *, si):
    h = ins[0][0]
    shift, scale, _ = _mod_rows(mod_ref, si)
    xb = (h * (1.0 + scale) + shift).astype(BF16)
    return [_dot(xb, w[...]) for w in sh]


def _proj(h_lat, h_ctx, mod, si, ws, joint=False):
    body = functools.partial(_proj_body, si=si)
    return _token_call(body, [h_lat], [h_ctx], ws, [w.shape[1] for w in ws], [F32] * len(ws),
                       mod=mod, with_ctx=True, name="mixer_in_proj", joint_out=joint)


def _outproj_body(ins, mod_ref, sh, scr, *, gi):
    h = ins[0][0]
    n_m = len(ins) - 1
    ws, g, b = sh[:n_m], sh[n_m], sh[n_m + 1]
    gate = mod_ref[0, gi:gi + 1, :]
    y = _dot(ins[1][0], ws[0][...])
    for k in range(1, n_m):
        y = y + _dot(ins[1 + k][0], ws[k][...])
    return [_layer_norm(ALPHA * h + gate * y, g[...], b[...])]


def _outproj_norm(h_lat, h_ctx, ms_lat, ms_ctx, ws, mod, gi, g, b, with_ctx=True):
    body = functools.partial(_outproj_body, gi=gi)
    seg = (h_lat.shape[1], h_ctx.shape[1])
    lat, ctx = _token_call(body, [h_lat] + ms_lat, [h_ctx] + ms_ctx, list(ws) + [g.reshape(1, -1), b.reshape(1, -1)],
                           [D_MODEL], [F32], mod=mod, with_ctx=with_ctx, name="mixer_out_proj_norm",
                           seg=seg if with_ctx else (h_lat.shape[1], 0))
    return lat[0], ctx[0]


def _chunk_call(body, pairs, tables, shared, out_widths, out_dtypes, *, chunk, rev, scratch, name):
    bsz, n_lat = pairs[0][0].shape[:2]
    n_ctx = pairs[0][1].shape[1]
    ncl, ncc = n_lat // chunk, n_ctx // chunk

    def lat_idx(n):
        i = (ncl - 1 - (n - ncc)) if rev else (n - ncc)
        return jnp.clip(i, 0, ncl - 1)

    def ctx_idx(n):
        i = (ncc - 1 - n) if rev else n
        return jnp.clip(i, 0, ncc - 1)

    in_specs, args = [], []
    for lat, ctx, w, cb in pairs:
        in_specs.append(pl.BlockSpec((1, chunk, w), lambda b, n, cb=cb: (b, lat_idx(n), cb)))
        in_specs.append(pl.BlockSpec((1, chunk, w), lambda b, n, cb=cb: (b, ctx_idx(n), cb)))
        args += [lat, ctx]
    for t in tables:
        in_specs.append(pl.BlockSpec((chunk, t.shape[1]), lambda b, n: (lat_idx(n), 0)))
        args.append(t)
    in_specs += [_full_spec(s) for s in shared]
    args += list(shared)
    out_specs, out_shape = [], []
    for w, dt in zip(out_widths, out_dtypes):
        out_specs.append(pl.BlockSpec((1, chunk, w), lambda b, n: (b, lat_idx(n), 0)))
        out_specs.append(pl.BlockSpec((1, chunk, w), lambda b, n: (b, ctx_idx(n), 0)))
        out_shape.append(jax.ShapeDtypeStruct((bsz, n_lat, w), dt))
        out_shape.append(jax.ShapeDtypeStruct((bsz, n_ctx, w), dt))
    n_p, n_t, n_s, n_o = len(pairs), len(tables), len(shared), len(out_widths)

    def kern(*refs):
        pos = 0
        pr = refs[pos:pos + 2 * n_p]; pos += 2 * n_p
        tr = refs[pos:pos + n_t]; pos += n_t
        sh = refs[pos:pos + n_s]; pos += n_s
        orf = refs[pos:pos + 2 * n_o]; pos += 2 * n_o
        scr = refs[pos:]
        n = pl.program_id(1)
        is_ctx = n < ncc
        vals = [jnp.where(is_ctx, pr[2 * i + 1][0], pr[2 * i][0]) for i in range(n_p)]
        outs = body(n, is_ctx, vals, [t[...] for t in tr], sh, scr)
        for i, v in enumerate(outs):
            lat_o, ctx_o = orf[2 * i], orf[2 * i + 1]

            @pl.when(is_ctx)
            def _(ctx_o=ctx_o, v=v):
                ctx_o[0] = v.astype(ctx_o.dtype)

            @pl.when(jnp.logical_not(is_ctx))
            def _(lat_o=lat_o, v=v):
                lat_o[0] = v.astype(lat_o.dtype)

    outs = pl.pallas_call(
        kern,
        grid=(bsz, ncl + ncc),
        in_specs=in_specs,
        out_specs=out_specs,
        out_shape=out_shape,
        scratch_shapes=scratch,
        compiler_params=_cparams("arbitrary", "arbitrary"),
        name=name,
    )(*args)
    return [(outs[2 * i], outs[2 * i + 1]) for i in range(n_o)]


def _chunk_call_all(body, ins, shared, out_widths, out_dtypes, *, seg, chunk, rev, scratch, name):
    n_lat, n_ctx = seg
    bsz = ins[0][0].shape[0]
    ncl, ncc = n_lat // chunk, n_ctx // chunk
    n_all = ncl + ncc

    def blk(n):
        if rev:
            return n_all - 1 - n
        return jnp.where(n < ncc, ncl + n, n - ncc)

    in_specs = [pl.BlockSpec((bsz, chunk, w), lambda n, cb=cb: (0, blk(n), cb)) for _, w, cb in ins]
    in_specs += [_full_spec(s) for s in shared]
    out_specs = [pl.BlockSpec((bsz, chunk, w), lambda n: (0, blk(n), 0)) for w in out_widths]
    out_shape = [jax.ShapeDtypeStruct((bsz, n_lat + n_ctx, w), dt) for w, dt in zip(out_widths, out_dtypes)]
    n_i, n_s, n_o = len(ins), len(shared), len(out_widths)

    def kern(*refs):
        in_r, sh = refs[:n_i], refs[n_i:n_i + n_s]
        out_r = refs[n_i + n_s:n_i + n_s + n_o]
        scr = refs[n_i + n_s + n_o:]
        outs = body(pl.program_id(0), [r[...] for r in in_r], sh, scr)
        for o, v in zip(out_r, outs):
            o[...] = v.astype(o.dtype)

    return pl.pallas_call(
        kern,
        grid=(n_all,),
        in_specs=in_specs,
        out_specs=out_specs,
        out_shape=out_shape,
        scratch_shapes=scratch,
        compiler_params=_cparams("arbitrary"),
        name=name,
    )(*[a for a, _, _ in ins], *shared)


S5_WIDTH = 256
S5_GROUP = 16
S5_GROUPS = 16
S5_STATE = 64
S5_T = 4
S5_ROW = S5_T * S5_WIDTH
S5_NSTATE = S5_GROUPS * S5_STATE


def _s5_operators(lam_re, lam_im, log_dt, b_re, b_im, c_re, c_im):
    t_len, g_n, p_n, c_n = S5_T, S5_GROUPS, S5_STATE, S5_GROUP
    dt = jnp.exp(log_dt)[..., None]
    a_r, a_i = lam_re * dt, lam_im * dt
    mag = jnp.exp(a_r)
    lb_re, lb_im = mag * jnp.cos(a_i), mag * jnp.sin(a_i)
    den = lam_re * lam_re + lam_im * lam_im
    nr = lb_re - 1.0
    coef_re = (nr * lam_re + lb_im * lam_im) / den
    coef_im = (lb_im * lam_re - nr * lam_im) / den
    bb_re = coef_re[..., None] * b_re - coef_im[..., None] * b_im
    bb_im = coef_re[..., None] * b_im + coef_im[..., None] * b_re

    def lam_pow(e):
        m = jnp.exp(a_r * e)
        return m * jnp.cos(a_i * e), m * jnp.sin(a_i * e)

    eye = jnp.eye(g_n, dtype=F32)

    def kern_tau(d, tau):
        pr, pi = lam_pow(float(tau))
        cp_re = c_re * pr[d][:, None, :] - c_im * pi[d][:, None, :]
        cp_im = c_re * pi[d][:, None, :] + c_im * pr[d][:, None, :]
        k = (jnp.einsum("gcp,gpe->gec", cp_re, bb_re[d], precision=HI)
             - jnp.einsum("gcp,gpe->gec", cp_im, bb_im[d], precision=HI))
        return k

    k0 = [kern_tau(0, tau) for tau in range(t_len)]
    k1 = [kern_tau(1, tau) for tau in range(t_len)]
    zero = jnp.zeros_like(k0[0])
    rows = []
    for s in range(t_len):
        cols = []
        for t in range(t_len):
            blk = zero
            if s <= t:
                blk = blk + k0[t - s]
            if s >= t:
                blk = blk + k1[s - t]
            cols.append(blk)
        rows.append(jnp.stack(cols, 0))
    kst = jnp.stack(rows, 0)
    m_op = jnp.einsum("stgec,gh->sgethc", kst, eye).reshape(S5_ROW, S5_ROW)

    w_ops, v_ops, lam_t = [], [], []
    for d in range(2):
        w_re, w_im, v_re, v_im = [], [], [], []
        for s in range(t_len):
            pr, pi = lam_pow(float(t_len - 1 - s) if d == 0 else float(s))
            w_re.append(pr[d][..., None] * bb_re[d] - pi[d][..., None] * bb_im[d])
            w_im.append(pr[d][..., None] * bb_im[d] + pi[d][..., None] * bb_re[d])
        for t in range(t_len):
            pr, pi = lam_pow(float(t + 1) if d == 0 else float(t_len - t))
            v_re.append(c_re * pr[d][:, None, :] - c_im * pi[d][:, None, :])
            v_im.append(-(c_re * pi[d][:, None, :] + c_im * pr[d][:, None, :]))
        for w in (w_re, w_im):
            w_ops.append(jnp.einsum("sgpe,gh->sgehp", jnp.stack(w, 0), eye).reshape(S5_ROW, S5_NSTATE))
        for v in (v_re, v_im):
            v_ops.append(jnp.einsum("tgcp,gh->gpthc", jnp.stack(v, 0), eye).reshape(S5_NSTATE, S5_ROW))
        pr, pi = lam_pow(float(t_len))
        lam_t.append((pr[d].reshape(1, S5_NSTATE), pi[d].reshape(1, S5_NSTATE)))
    w_cat = jnp.concatenate([m_op] + w_ops, axis=1).astype(BF16)
    return w_cat, [v.astype(BF16) for v in v_ops], lam_t


def _s5_drive_body(ins, mod_ref, sh, scr):
    xb = ins[0][0].astype(BF16)
    w = sh[0]
    n = w.shape[1] // S5_ROW
    return [_dot(xb, w[:, i * S5_ROW:(i + 1) * S5_ROW]) for i in range(n)]


def _s5_state_scan(b_re, b_im, lam, rev):
    (bre_l, bre_c), (bim_l, bim_c) = b_re, b_im
    bsz, nl, w = bre_l.shape
    nc = bre_c.shape[1]

    def flat(a):
        return a.reshape(bsz * a.shape[1], w)

    def kern(brl, bil, brc, bic, lr_ref, li_ref, hrl, hil, hrc, hic):
        lr = jnp.broadcast_to(lr_ref[...], (bsz, LANE))
        li = jnp.broadcast_to(li_ref[...], (bsz, LANE))

        def make(br, bi, hr_o, hi_o, nrows):
            def step(i, carry):
                hr, hi = carry
                n = (nrows - 1 - i) if rev else i
                idx = pl.ds(n, bsz, stride=nrows)
                hr_o[idx, :] = hr
                hi_o[idx, :] = hi
                xr, xi = br[idx, :], bi[idx, :]
                return lr * hr - li * hi + xr, lr * hi + li * hr + xi
            return step

        z = jnp.zeros((bsz, LANE), F32)
        carry = lax.fori_loop(0, nc, make(brc, bic, hrc, hic, nc), (z, z), unroll=8)
        lax.fori_loop(0, nl, make(brl, bil, hrl, hil, nl), carry, unroll=8)

    def spec(rows):
        return pl.BlockSpec((rows, LANE), lambda j: (0, j))

    lam_spec = pl.BlockSpec((1, LANE), lambda j: (0, j))
    outs = pl.pallas_call(
        kern,
        grid=(w // LANE,),
        in_specs=[spec(bsz * nl), spec(bsz * nl), spec(bsz * nc), spec(bsz * nc), lam_spec, lam_spec],
        out_specs=[spec(bsz * nl), spec(bsz * nl), spec(bsz * nc), spec(bsz * nc)],
        out_shape=[jax.ShapeDtypeStruct((bsz * nl, w), F32), jax.ShapeDtypeStruct((bsz * nl, w), F32),
                   jax.ShapeDtypeStruct((bsz * nc, w), F32), jax.ShapeDtypeStruct((bsz * nc, w), F32)],
        compiler_params=_cparams("arbitrary"),
        name="s5_state_scan",
    )(flat(bre_l), flat(bim_l), flat(bre_c), flat(bim_c), lam[0], lam[1])
    hrl, hil, hrc, hic = outs
    return (hrl.reshape(bsz, nl, w), hrc.reshape(bsz, nc, w)), (hil.reshape(bsz, nl, w), hic.reshape(bsz, nc, w))


def _s5_finish_body(ins, mod_ref, sh, scr):
    yi, h0r, h0i, h1r, h1i, u = [r[0] for r in ins]
    v0r, v0i, v1r, v1i, d_skip, glu_w, glu_b = sh
    y = yi + d_skip[...] * u
    for h, v in ((h0r, v0r), (h0i, v0i), (h1r, v1r), (h1i, v1i)):
        y = y + _dot(h.astype(BF16), v[...])
    z = jax.nn.gelu(y)
    return [z * jax.nn.sigmoid(_dot(z.astype(BF16), glu_w[...]) + glu_b[...])]


def _s5_branch(u_lat, u_ctx, ops, d_skip, glu_w, glu_b):
    w_cat, v_ops, lam_t = ops
    bsz, n_lat, _ = u_lat.shape
    n_ctx = u_ctx.shape[1]
    u4l = u_lat.reshape(bsz, n_lat // S5_T, S5_ROW)
    u4c = u_ctx.reshape(bsz, n_ctx // S5_T, S5_ROW)
    lat, ctx = _token_call(_s5_drive_body, [u4l], [u4c], [w_cat], [S5_ROW] * 5, [F32] * 5,
                           mod=None, with_ctx=True, name="s5_drive")
    pairs = list(zip(lat, ctx))
    h0 = _s5_state_scan(pairs[1], pairs[2], lam_t[0], rev=False)
    h1 = _s5_state_scan(pairs[3], pairs[4], lam_t[1], rev=True)
    eye_t = jnp.eye(S5_T, dtype=F32)
    glu_k = jnp.kron(eye_t, glu_w).astype(BF16)
    fin_l = [pairs[0][0], h0[0][0], h0[1][0], h1[0][0], h1[1][0], u4l]
    fin_c = [pairs[0][1], h0[0][1], h0[1][1], h1[0][1], h1[1][1], u4c]
    shared = list(v_ops) + [jnp.tile(d_skip, S5_T).reshape(1, S5_ROW), glu_k, jnp.tile(glu_b, S5_T).reshape(1, S5_ROW)]
    lat, ctx = _token_call(_s5_finish_body, fin_l, fin_c, shared, [S5_ROW], [BF16],
                           mod=None, with_ctx=True, name="s5_finish")
    return lat[0].reshape(bsz, n_lat, S5_WIDTH), ctx[0].reshape(bsz, n_ctx, S5_WIDTH)


RET_HEADS = 6
RET_HEAD_DIM = 128
RET_WIDTH = RET_HEADS * RET_HEAD_DIM
RET_CHUNK = 256
ROPE_BASE = 10000.0


def _rope_tables(n_lat):
    nf = RET_HEAD_DIM // 4
    rows = n_lat // GRID_W
    freqs = ROPE_BASE ** (-jnp.arange(nf, dtype=F32) / nf)
    pr = jnp.broadcast_to(jnp.arange(rows, dtype=F32)[:, None], (rows, GRID_W)).reshape(-1)
    pc = jnp.broadcast_to(jnp.arange(GRID_W, dtype=F32)[None, :], (rows, GRID_W)).reshape(-1)
    ang = jnp.concatenate([pr[:, None] * freqs, pc[:, None] * freqs], -1)
    cos, sin = jnp.cos(ang), jnp.sin(ang)
    return jnp.concatenate([cos, cos], -1), jnp.concatenate([-sin, sin], -1)


def _ret_tables(log_rate):
    c = RET_CHUNK
    lg = -jnp.exp(log_rate.astype(F32))
    idx = jnp.arange(c, dtype=F32)
    diff = idx[:, None] - idx[None, :]
    past, fut = diff >= 0, diff <= 0
    d0 = jnp.where(past, jnp.exp(jnp.where(past, diff, 0.0)[None] * lg[0][:, None, None]), 0.0)
    d1 = jnp.where(fut, jnp.exp(jnp.where(fut, -diff, 0.0)[None] * lg[1][:, None, None]), 0.0)

    def rep(t):
        return jnp.repeat(t, RET_HEAD_DIM, axis=-1)

    fwd = (rep(jnp.exp((idx + 1.0)[:, None] * lg[0])), rep(jnp.exp((c - 1.0 - idx)[:, None] * lg[0])),
           rep(jnp.exp(c * lg[0])[None]))
    bwd = (rep(jnp.exp((c - idx)[:, None] * lg[1])), rep(jnp.exp(idx[:, None] * lg[1])),
           rep(jnp.exp(c * lg[1])[None]))
    return d0 + d1, fwd, bwd


def _rope(x, cos, sin):
    return x * cos + pltpu.roll(x, RET_HEAD_DIM // 2, 1) * sin


def _ret_heads(is_ctx, q, k, cos, sin, h):
    sl = slice(h * RET_HEAD_DIM, (h + 1) * RET_HEAD_DIM)
    qh, kh = q[:, sl], k[:, sl]
    qh = jnp.where(is_ctx, qh, _rope(qh, cos, sin))
    kh = jnp.where(is_ctx, kh, _rope(kh, cos, sin)) * (RET_HEAD_DIM ** -0.5)
    return sl, qh, kh


def _ret_bwd_body(n, is_ctx, vals, tabs, sh, scr):
    q, k, v = vals
    cos, sin = tabs
    xi, zeta, gc = sh
    state = scr[0]

    @pl.when(n == 0)
    def _():
        state[...] = jnp.zeros_like(state)

    outs = []
    for h in range(RET_HEADS):
        sl, qh, kh = _ret_heads(is_ctx, q, k, cos, sin, h)
        s_old = state[h]
        outs.append(_dot((qh * xi[:, sl]).astype(BF16), s_old.astype(BF16)))
        state[h] = gc[:, sl] * s_old + _dot_tn((kh * zeta[:, sl]).astype(BF16), v[:, sl].astype(BF16))
    return [jnp.concatenate(outs, -1)]


def _ret_fwd_body(n, is_ctx, vals, tabs, sh, scr):
    q, k, v, g, o_bwd = vals
    cos, sin = tabs
    dcomb, xi, zeta, gc = sh
    state = scr[0]

    @pl.when(n == 0)
    def _():
        state[...] = jnp.zeros_like(state)

    outs = []
    for h in range(RET_HEADS):
        sl, qh, kh = _ret_heads(is_ctx, q, k, cos, sin, h)
        vb = v[:, sl].astype(BF16)
        s_old = state[h]
        scores = _dot_nt(qh.astype(BF16), kh.astype(BF16)) * dcomb[h]
        o = (_dot(scores.astype(BF16), vb) + _dot((qh * xi[:, sl]).astype(BF16), s_old.astype(BF16))
             + o_bwd[:, sl])
        state[h] = gc[:, sl] * s_old + _dot_tn((kh * zeta[:, sl]).astype(BF16), vb)
        mu = jnp.mean(o, -1, keepdims=True)
        oc = o - mu
        var = jnp.mean(oc * oc, -1, keepdims=True)
        gh = g[:, sl]
        outs.append(gh * jax.nn.sigmoid(gh) * (oc * lax.rsqrt(var + LN_EPS)))
    return [jnp.concatenate(outs, -1)]


def _retention_branch(p_lat, p_ctx, log_rate):
    n_lat = p_lat.shape[1]
    cos, sin = _rope_tables(n_lat)
    dcomb, fwd, bwd = _ret_tables(log_rate)
    w = RET_WIDTH
    state = [pltpu.VMEM((RET_HEADS, RET_HEAD_DIM, RET_HEAD_DIM), F32)]
    qkv = [(p_lat, p_ctx, w, i) for i in range(3)]
    (o_bwd,) = _chunk_call(_ret_bwd_body, qkv, [cos, sin], list(bwd), [w], [F32],
                           chunk=RET_CHUNK, rev=True, scratch=state, name="retention_bwd")
    pairs = qkv + [(p_lat, p_ctx, w, 3), (o_bwd[0], o_bwd[1], w, 0)]
    (r,) = _chunk_call(_ret_fwd_body, pairs, [cos, sin], [dcomb] + list(fwd), [w], [BF16],
                       chunk=RET_CHUNK, rev=False, scratch=state, name="retention_fwd")
    return r


def _mixer_ab(h_lat, h_ctx, mod, w_in, w_out, s5_ops, s5_d, s5_glu_w, s5_glu_b, ret_log_rate):
    w_u = w_in[:, :S5_WIDTH].astype(BF16)
    w_qkvg = w_in[:, S5_WIDTH:].astype(BF16)
    (p_lat, u_lat), (p_ctx, u_ctx) = _proj(h_lat, h_ctx, mod, 3, [w_qkvg, w_u])
    a_lat, a_ctx = _s5_branch(u_lat, u_ctx, s5_ops, s5_d, s5_glu_w, s5_glu_b)
    r_lat, r_ctx = _retention_branch(p_lat, p_ctx, ret_log_rate)
    return [a_lat, r_lat], [a_ctx, r_ctx], [w_out[:S5_WIDTH].astype(BF16), w_out[S5_WIDTH:].astype(BF16)]


def _mm(a, b, exact=False):
    if exact:
        return jnp.dot(a, b, preferred_element_type=F32, precision=HI)
    return _dot(a.astype(BF16), b.astype(BF16))


def _mm_nt(a, b):
    return _dot_nt(a.astype(BF16), b.astype(BF16))


def _mm_tn(a, b):
    return _dot_tn(a.astype(BF16), b.astype(BF16))


def _softplus(x):
    return jnp.maximum(x, 0.0) + jnp.log1p(jnp.exp(-jnp.abs(x)))


def _shift_rows(x, s, prev8, next8):
    rows = x.shape[0]
    rolled = pltpu.roll(x, (-s) % rows, 0)
    r8 = lax.broadcasted_iota(jnp.int32, (SUBLANE, x.shape[1]), 0)
    if s < 0:
        edge = jnp.where(r8 < -s, pltpu.roll(prev8, -s, 0), rolled[:SUBLANE])
        return jnp.concatenate([edge, rolled[SUBLANE:]], 0)
    edge = jnp.where(r8 >= SUBLANE - s, pltpu.roll(next8, SUBLANE - s, 0), rolled[rows - SUBLANE:])
    return jnp.concatenate([rolled[:rows - SUBLANE], edge], 0)


def _halo_rows(halo, i, width):
    if halo is None:
        z = jnp.zeros((SUBLANE, width), F32)
        return z, z
    prev_r, next_r, has_prev, has_next = halo
    return (jnp.where(has_prev, prev_r[i][0], 0.0), jnp.where(has_next, next_r[i][0], 0.0))


def _order_masks(c, rev, reps=1):
    ri = lax.broadcasted_iota(jnp.int32, (c, reps * c), 0)
    ci = jnp.bitwise_and(lax.broadcasted_iota(jnp.int32, (c, reps * c), 1), c - 1)
    if rev:
        return ri <= ci, ri < ci
    return ri >= ci, ri > ci


TRI_BASE = 8


def _bmm(a, b):
    return lax.dot_general(a.astype(BF16), b.astype(BF16), (((2,), (1,)), ((0,), (0,))),
                           preferred_element_type=F32)


def _bmm_nt(a, b):
    return lax.dot_general(a.astype(BF16), b.astype(BF16), (((2,), (2,)), ((0,), (0,))),
                           preferred_element_type=F32)


def _bmm_tn(a, b):
    return lax.dot_general(a.astype(BF16), b.astype(BF16), (((1,), (1,)), ((0,), (0,))),
                           preferred_element_type=F32)


def _tri_inverse(n_mat, mul):
    c, lanes = n_mat.shape[1], n_mat.shape[2]
    ri = lax.broadcasted_iota(jnp.int32, (c, lanes), 0)
    ci = jnp.bitwise_and(lax.broadcasted_iota(jnp.int32, (c, lanes), 1), c - 1)

    def same_block(bits):
        return jnp.right_shift(ri, bits) == jnp.right_shift(ci, bits)

    bits = TRI_BASE.bit_length() - 1
    same = same_block(bits)
    p = jnp.where(same, n_mat, 0.0)
    x = jnp.where(ri == ci, 1.0, 0.0) + p
    k = 2
    while k < TRI_BASE:
        p = mul(p, p)
        x = x + mul(x, p)
        k *= 2
    while (1 << bits) < c:
        bits += 1
        same2 = same_block(bits)
        off = jnp.where(jnp.logical_and(same2, jnp.logical_not(same)), n_mat, 0.0)
        x = x + mul(mul(x, off), x)
        same = same2
    return x


def _heads_to_batch(x, n):
    w = x.shape[2] // n
    return jnp.concatenate([x[:, :, i * w:(i + 1) * w] for i in range(n)], axis=0)


def _batch_to_heads(x, n):
    b = x.shape[0] // n
    return jnp.concatenate([x[i * b:(i + 1) * b] for i in range(n)], axis=-1)


def _scan_cumsum(tri, x):
    bsz, _, w = x.shape
    flat = jnp.concatenate([x[b] for b in range(bsz)], axis=-1)
    cum = _mm(tri, flat, exact=True)
    return jnp.stack([cum[:, b * w:(b + 1) * w] for b in range(bsz)], axis=0)


def _seg_sum(x, seg):
    hi = x.astype(BF16)
    lo = (x - hi.astype(F32)).astype(BF16)
    return _dot(hi, seg) + _dot(lo, seg)


GDN_HEADS = 4
GDN_HEAD_DIM = 128
GDN_WIDTH = GDN_HEADS * GDN_HEAD_DIM
GDN_CONV = 5
GDN_CHUNK = 64


def _gdn_prep_body(ins, mod_ref, sh, scr, halo):
    qkv, sm = ins[0][0], ins[1][0]
    conv_w, a_neg, dt_bias = sh
    prev8, next8 = _halo_rows(halo, 0, qkv.shape[1])
    half = GDN_CONV // 2
    acc = conv_w[half:half + 1, :] * qkv
    for i in range(GDN_CONV):
        if i != half:
            acc = acc + conv_w[i:i + 1, :] * _shift_rows(qkv, i - half, prev8, next8)
    y = acc * jax.nn.sigmoid(acc)
    outs = []
    for part, scale in ((0, GDN_HEAD_DIM ** -0.5), (1, 1.0)):
        cols = []
        for h in range(GDN_HEADS):
            lo = part * GDN_WIDTH + h * GDN_HEAD_DIM
            t = y[:, lo:lo + GDN_HEAD_DIM]
            cols.append(t * (lax.rsqrt(jnp.sum(t * t, -1, keepdims=True) + 1e-6) * scale))
        outs.append(jnp.concatenate(cols, -1))
    outs.append(y[:, 2 * GDN_WIDTH:])
    lane = lax.broadcasted_iota(jnp.int32, sm.shape, 1)
    g = a_neg[...] * _softplus(sm + dt_bias[...])
    outs.append(jnp.where(lane < 2 * GDN_HEADS, g, jax.nn.sigmoid(sm)))
    return outs


def _gdn_body(n, vals, sh, scr, *, rev, last):
    q, k, v, sm = vals[:4]
    state = scr[0]
    c = GDN_CHUNK
    bsz = q.shape[0]

    @pl.when(n == 0)
    def _():
        state[...] = jnp.zeros_like(state)

    later, strict = _order_masks(c, rev)
    gcum = _scan_cumsum(later.astype(F32), sm)
    gcum_t = [gcum[b].T for b in range(bsz)]
    end = 0 if rev else c - 1
    d = 1 if rev else 0
    g_col, g_row, beta = [], [], []
    for h in range(GDN_HEADS):
        col = d * GDN_HEADS + h
        bcol = 2 * GDN_HEADS + col
        for b in range(bsz):
            g_col.append(gcum[b][:, col:col + 1])
            g_row.append(gcum_t[b][col:col + 1, :])
            beta.append(sm[b][:, bcol:bcol + 1])
    gc, gr, beta = jnp.stack(g_col), jnp.stack(g_row), jnp.stack(beta)
    dec = jnp.where(later, jnp.exp(jnp.where(later, gc - gr, 0.0)), 0.0)
    qg, kg, vg = (_heads_to_batch(t, GDN_HEADS) for t in (q, k, v))
    kb = kg * beta
    scores = _bmm_nt(jnp.concatenate([kb, qg], axis=1), kg)
    a_mat = jnp.where(strict, scores[:, :c] * dec, 0.0)
    attn = scores[:, c:] * dec
    t_mat = _tri_inverse(-a_mat, _bmm)
    eg = jnp.exp(gc)
    wu = _bmm(t_mat, jnp.concatenate([kb * eg, vg * beta], axis=-1))
    w, u = wu[:, :, :GDN_HEAD_DIM], wu[:, :, GDN_HEAD_DIM:]
    g_end = gc[:, end:end + 1, :]
    s_old = state[...]
    ws = _bmm(jnp.concatenate([w, qg * eg], axis=1), s_old)
    v_new = u - ws[:, :c]
    o = ws[:, c:] + _bmm(attn, v_new)
    state[...] = s_old * jnp.exp(g_end) + _bmm_tn(kg * jnp.exp(g_end - gc), v_new)
    o = _batch_to_heads(o, GDN_HEADS)
    if not last:
        return [o]
    z, o_prev = vals[4], vals[5]
    norm_w = sh[0]
    o = o + o_prev
    cols = []
    for h in range(GDN_HEADS):
        t = o[:, :, h * GDN_HEAD_DIM:(h + 1) * GDN_HEAD_DIM]
        cols.append(t * lax.rsqrt(jnp.mean(t * t, -1, keepdims=True) + 1e-6) * norm_w[...])
    return [jnp.concatenate(cols, -1) * (z * jax.nn.sigmoid(z))]


def _deltanet_branch(qkv, z, sm, seg, conv_w, a_log, dt_bias, norm_w):
    bsz = qkv.shape[0]
    pad = sm.shape[2] - 2 * GDN_HEADS
    a_neg = jnp.pad(-jnp.exp(a_log.astype(F32)).reshape(1, -1), ((0, 0), (0, pad)))
    dtb = jnp.pad(dt_bias.astype(F32).reshape(1, -1), ((0, 0), (0, pad)))
    w = GDN_WIDTH
    prep = _token_call(_gdn_prep_body, [qkv, sm], [qkv, sm], [conv_w, a_neg, dtb],
                       [w, w, w, sm.shape[2]], [F32] * 4, mod=None, with_ctx=True,
                       name="gdn_prep", n_halo=1, seg=seg, joint_out=True)
    ins = [(a, a.shape[2], 0) for a in prep]
    state = [pltpu.VMEM((GDN_HEADS * bsz, GDN_HEAD_DIM, GDN_HEAD_DIM), F32)]
    (o_bwd,) = _chunk_call_all(functools.partial(_gdn_body, rev=True, last=False), ins, [], [w], [F32],
                               seg=seg, chunk=GDN_CHUNK, rev=True, scratch=state, name="gdn_bwd")
    ins = ins + [(z, w, 0), (o_bwd, w, 0)]
    (out,) = _chunk_call_all(functools.partial(_gdn_body, rev=False, last=True), ins, [norm_w.reshape(1, 1, -1)],
                             [w], [BF16], seg=seg, chunk=GDN_CHUNK, rev=False, scratch=state, name="gdn_fwd")
    return out


RWKV_HEADS = 8
RWKV_HEAD_DIM = 64
RWKV_WIDTH = RWKV_HEADS * RWKV_HEAD_DIM
RWKV_DECAY_LORA = 32
RWKV_AAA_LORA = 32
RWKV_GATE_LORA = 96
RWKV_GN_EPS = 64e-5
RWKV_CHUNK = 64
RWKV_PAIRS = RWKV_WIDTH // LANE


def _rwkv_prep_body(ins, mod_ref, sh, scr, halo):
    rkv, sm = ins[0][0], ins[1][0]
    mu_main, mu_sm, k_k, k_a, r_k, w0, a0, lora_w, seg = sh
    w = RWKV_WIDTH

    def lerp(x, mu, i):
        prev8, next8 = _halo_rows(halo, i, x.shape[1])
        xs = 0.5 * (_shift_rows(x, -1, prev8, next8) + _shift_rows(x, 1, prev8, next8))
        return x + mu[...] * (xs - x)

    rkv = lerp(rkv, mu_main, 0)
    sm = lerp(sm, mu_sm, 1)
    r, k, v = rkv[:, :w], rkv[:, w:2 * w], rkv[:, 2 * w:]
    lane = lax.broadcasted_iota(jnp.int32, sm.shape, 1)
    n_dec, n_aaa = 2 * RWKV_DECAY_LORA, 2 * RWKV_AAA_LORA
    t = jnp.where(lane < n_dec, jnp.tanh(sm), jnp.where(lane < n_dec + n_aaa, sm, jax.nn.sigmoid(sm)))
    lo = _dot(t.astype(BF16), lora_w[...])
    kk = k * k_k[...]
    kk = kk * lax.rsqrt(_seg_sum(kk * kk, seg[...]) + 1e-6)
    outs = [r, v, kk]
    k_sum = None
    for d in range(2):
        w_log = -_softplus(-(w0[d:d + 1, :] + lo[:, d * w:(d + 1) * w])) - 0.5
        a = jax.nn.sigmoid(a0[d:d + 1, :] + lo[:, (2 + d) * w:(3 + d) * w])
        k_d = k * (1.0 + (a - 1.0) * k_a[...])
        outs += [k_d, -jnp.exp(w_log), kk * a]
        k_sum = k_d if k_sum is None else k_sum + k_d
    outs.append(lo[:, 4 * w:])
    outs.append(_seg_sum(r * k_sum * r_k[...], seg[...]) * v)
    return outs


def _head_rows(x):
    first = lax.broadcasted_iota(jnp.int32, x.shape[1:], 1) < RWKV_HEAD_DIM
    return jnp.concatenate([jnp.where(first, x, 0.0), jnp.where(first, 0.0, x)], axis=1)


def _pair_mul(x, y):
    return _bmm(x, _head_rows(y))


def _rwkv_body(n, vals, sh, scr, *, rev, last):
    r, k, v, kk, lw, b = vals[:6]
    state = scr[0]
    c = RWKV_CHUNK
    bsz = r.shape[0]

    @pl.when(n == 0)
    def _():
        state[...] = jnp.zeros_like(state)

    later, _ = _order_masks(c, rev)
    later2, strict2 = _order_masks(c, rev, reps=2)
    cum = _scan_cumsum(later.astype(F32), lw)
    end = 0 if rev else c - 1
    cum_end = cum[:, end:end + 1, :]
    e_inv = jnp.exp(-cum)
    e_tail = jnp.exp(cum_end - cum)
    to_pairs = functools.partial(_heads_to_batch, n=RWKV_PAIRS)
    r_t = to_pairs(r * jnp.exp(cum))
    a_t = to_pairs(-kk * jnp.exp(cum - lw))
    b_t, k_t = to_pairs(b * e_inv), to_pairs(k * e_inv)
    b_h, k_h = to_pairs(b * e_tail), to_pairs(k * e_tail)
    p_end = to_pairs(jnp.exp(cum_end))
    v_p = to_pairs(v)
    prods = _bmm_nt(jnp.concatenate([a_t, r_t], axis=1),
                    jnp.concatenate([_head_rows(b_t), _head_rows(k_t)], axis=1))
    a_ab = jnp.where(strict2, prods[:, :c, :LANE], 0.0)
    a_ak = jnp.where(strict2, prods[:, :c, LANE:], 0.0)
    a_rb = jnp.where(later2, prods[:, c:, :LANE], 0.0)
    a_rk = jnp.where(later2, prods[:, c:, LANE:], 0.0)
    inv = _tri_inverse(a_ab, _pair_mul)
    akv = _pair_mul(a_ak, v_p)
    wu = _bmm(inv, jnp.concatenate([_head_rows(a_t), _head_rows(akv)], axis=-1))
    w_t, u_t = wu[:, :, :LANE], wu[:, :, LANE:]
    s_old = state[...]
    ws = _bmm_nt(jnp.concatenate([w_t, r_t], axis=1), s_old)
    u = ws[:, :c] + u_t
    y = ws[:, c:] + _bmm(jnp.concatenate([a_rb, a_rk], axis=-1),
                         jnp.concatenate([_head_rows(u), _head_rows(v_p)], axis=1))
    s_new = s_old * p_end + _bmm_tn(jnp.concatenate([u, v_p], axis=1), jnp.concatenate([b_h, k_h], axis=1))
    bi = lax.broadcasted_iota(jnp.int32, (LANE, LANE), 0) < RWKV_HEAD_DIM
    bj = lax.broadcasted_iota(jnp.int32, (LANE, LANE), 1) < RWKV_HEAD_DIM
    state[...] = jnp.where(bi == bj, s_new, 0.0)
    y = _batch_to_heads(y, RWKV_PAIRS)
    if not last:
        return [y]
    y_prev, g, bonus = vals[6:9]
    ln_w, ln_b, seg = sh
    w = RWKV_WIDTH
    y = (y + y_prev).reshape(bsz * c, w)
    inv_n = 1.0 / RWKV_HEAD_DIM
    mu = _seg_sum(y, seg[...]) * inv_n
    yc = y - mu
    var = _seg_sum(yc * yc, seg[...]) * inv_n
    out = (yc * lax.rsqrt(var + RWKV_GN_EPS)) * ln_w[...] + ln_b[...]
    return [(out.reshape(bsz, c, w) + bonus) * g]


def _rwkv7_branch(rkv, sm, seg_len, mu, w0, w_up, a0, a_up, g_up, k_k, k_a, r_k, ln_w, ln_b):
    w = RWKV_WIDTH
    bsz = rkv.shape[0]
    n_sm = sm.shape[2]
    n_used = 2 * RWKV_DECAY_LORA + 2 * RWKV_AAA_LORA + RWKV_GATE_LORA
    mu_main = mu[:3 * w].reshape(1, -1)
    mu_sm = jnp.pad(mu[3 * w:], (0, n_sm - n_used)).reshape(1, -1)
    lora_w = jnp.zeros((n_sm, 5 * w), F32)
    for d in range(2):
        lo = d * RWKV_DECAY_LORA
        lora_w = lora_w.at[lo:lo + RWKV_DECAY_LORA, d * w:(d + 1) * w].set(w_up[d])
        lo = 2 * RWKV_DECAY_LORA + d * RWKV_AAA_LORA
        lora_w = lora_w.at[lo:lo + RWKV_AAA_LORA, (2 + d) * w:(3 + d) * w].set(a_up[d])
    lo = 2 * RWKV_DECAY_LORA + 2 * RWKV_AAA_LORA
    lora_w = lora_w.at[lo:lo + RWKV_GATE_LORA, 4 * w:].set(g_up).astype(BF16)
    head = jnp.arange(w) // RWKV_HEAD_DIM
    seg = (head[:, None] == head[None, :]).astype(BF16)
    row = lambda t: t.reshape(1, -1)
    shared = [mu_main, mu_sm, row(k_k), row(k_a), row(r_k), w0, a0, lora_w, seg]
    prep = _token_call(_rwkv_prep_body, [rkv, sm], [rkv, sm], shared, [w] * 11, [F32] * 11, mod=None,
                       with_ctx=True, name="rwkv_prep", n_halo=2, seg=seg_len, joint_out=True)
    pr = [(a, w, 0) for a in prep]
    state = [pltpu.VMEM((RWKV_PAIRS * bsz, LANE, LANE), F32)]
    (y_bwd,) = _chunk_call_all(functools.partial(_rwkv_body, rev=True, last=False),
                               [pr[0], pr[6], pr[1], pr[2], pr[7], pr[8]], [], [w], [F32],
                               seg=seg_len, chunk=RWKV_CHUNK, rev=True, scratch=state, name="rwkv_bwd")
    ins = [pr[0], pr[3], pr[1], pr[2], pr[4], pr[5], (y_bwd, w, 0), pr[9], pr[10]]
    (out,) = _chunk_call_all(functools.partial(_rwkv_body, rev=False, last=True), ins,
                             [row(ln_w), row(ln_b), seg], [w], [BF16],
                             seg=seg_len, chunk=RWKV_CHUNK, rev=False, scratch=state, name="rwkv_fwd")
    return out


def _pad_cols(w, n):
    return jnp.pad(w, ((0, 0), (0, n - w.shape[1])))


def _mixer_cd(h_lat, h_ctx, mod, w_in, w_out, conv_w, a_log, dt_bias, norm_w, mu, w0, w_up, a0, a_up, g_up,
              k_k, k_a, r_k, ln_w, ln_b):
    n_qkv, n_gsm = 3 * GDN_WIDTH, 4 * GDN_HEADS
    n_rkv = 3 * RWKV_WIDTH
    lo_z, lo_g = n_qkv, n_qkv + GDN_WIDTH
    lo_r = lo_g + n_gsm
    lo_s = lo_r + n_rkv
    ws = [w_in[:, :n_qkv], w_in[:, lo_z:lo_g], _pad_cols(w_in[:, lo_g:lo_r], LANE),
          w_in[:, lo_r:lo_s], _pad_cols(w_in[:, lo_s:], 2 * LANE)]
    seg = (h_lat.shape[1], h_ctx.shape[1])
    p = _proj(h_lat, h_ctx, mod, 3, [w.astype(BF16) for w in ws], joint=True)
    d = _deltanet_branch(p[0], p[1], p[2], seg, conv_w, a_log, dt_bias, norm_w)
    r = _rwkv7_branch(p[3], p[4], seg, mu, w0, w_up, a0, a_up, g_up, k_k, k_a, r_k, ln_w, ln_b)
    return [d, r], [d, r], [w_out[:GDN_WIDTH].astype(BF16), w_out[GDN_WIDTH:].astype(BF16)]


def kernel(x, c, ctx, c_ctx, ada_w, ada_b, ffn_w_in, ffn_w_out, ln_g, ln_b, ab_w_in, ab_w_out, s5_lam_re, s5_lam_im, s5_log_dt, s5_b_re, s5_b_im, s5_c_re, s5_c_im, s5_d, s5_glu_w, s5_glu_b, ret_log_rate, cd_w_in, cd_w_out, gdn_conv_w, gdn_a_log, gdn_dt_bias, gdn_norm_w, rwkv_mu, rwkv_w0, rwkv_w_up, rwkv_a0, rwkv_a_up, rwkv_g_up, rwkv_k_k, rwkv_k_a, rwkv_r_k, rwkv_ln_w, rwkv_ln_b):
    mods = _adaln(c, c_ctx, ada_w, ada_b)
    h_lat, h_ctx = x, ctx
    for i in range(DEPTH):
        mod = mods[i]
        j = i // 2
        keep_ctx = i < DEPTH - 1
        h_lat, h_ctx = _ffn(h_lat, h_ctx, mod, 0, ffn_w_in[i, 0].astype(BF16), ffn_w_out[i, 0].astype(BF16),
                            ln_g[i, 0], ln_b[i, 0])
        if i % 2 == 0:
            s5_ops = _s5_operators(s5_lam_re[j], s5_lam_im[j], s5_log_dt[j], s5_b_re[j], s5_b_im[j],
                                   s5_c_re[j], s5_c_im[j])
            ms_lat, ms_ctx, w_out = _mixer_ab(h_lat, h_ctx, mod, ab_w_in[j], ab_w_out[j], s5_ops, s5_d[j],
                                              s5_glu_w[j], s5_glu_b[j], ret_log_rate[j])
        else:
            ms_lat, ms_ctx, w_out = _mixer_cd(
                h_lat, h_ctx, mod, cd_w_in[j], cd_w_out[j], gdn_conv_w[j], gdn_a_log[j], gdn_dt_bias[j],
                gdn_norm_w[j], rwkv_mu[j], rwkv_w0[j], rwkv_w_up[j], rwkv_a0[j], rwkv_a_up[j], rwkv_g_up[j],
                rwkv_k_k[j], rwkv_k_a[j], rwkv_r_k[j], rwkv_ln_w[j], rwkv_ln_b[j])
        h_lat, h_ctx = _outproj_norm(h_lat, h_ctx, ms_lat, ms_ctx, w_out, mod, 5, ln_g[i, 1], ln_b[i, 1],
                                     with_ctx=keep_ctx)
        h_lat, h_ctx = _ffn(h_lat, h_ctx, mod, 6, ffn_w_in[i, 1].astype(BF16), ffn_w_out[i, 1].astype(BF16),
                            ln_g[i, 2], ln_b[i, 2], with_ctx=keep_ctx)
    return h_lat
```

```python
import functools
import math

import jax
import jax.numpy as jnp
from jax import lax
from jax.experimental import pallas as pl
from jax.experimental.pallas import tpu as pltpu

F32 = jnp.float32
BF16 = jnp.bfloat16
HI = lax.Precision.HIGHEST

D_MODEL = 1024
DEPTH = 4
N_MOD = 9
D_FF = 2816
MACARON = 0.5
ALPHA = (2.0 * DEPTH) ** 0.25
LN_EPS = 1e-5
GRID_W = 64

LANE = 128
SUBLANE = 8
MXU_TILE = 256
VMEM_LIMIT_BYTES = 56 * 1024 * 1024

TOKEN_ROWS = 512
FF_CHUNK = MXU_TILE


def _cparams(*sem):
    return pltpu.CompilerParams(dimension_semantics=sem, vmem_limit_bytes=VMEM_LIMIT_BYTES)


def _full_spec(a):
    nd = a.ndim
    return pl.BlockSpec(a.shape, lambda *_, nd=nd: (0,) * nd, pipeline_mode=pl.Buffered(1))


def _dot(a, b):
    return jnp.dot(a, b, preferred_element_type=F32)


def _dot_nt(a, b):
    return lax.dot_general(a, b, (((1,), (1,)), ((), ())), preferred_element_type=F32)


def _dot_tn(a, b):
    return lax.dot_general(a, b, (((0,), (0,)), ((), ())), preferred_element_type=F32)


def _layer_norm(x, g, b):
    mu = jnp.mean(x, -1, keepdims=True)
    xc = x - mu
    var = jnp.mean(xc * xc, -1, keepdims=True)
    return xc * lax.rsqrt(var + LN_EPS) * g + b


def _adaln_kernel(s_ref, w_ref, b_ref, o_ref):
    s = s_ref[...]
    s = s * jax.nn.sigmoid(s)
    o_ref[0] = jnp.dot(s, w_ref[0], preferred_element_type=F32, precision=HI) + b_ref[0]


def _adaln(c, c_ctx, ada_w, ada_b):
    bsz = c.shape[0]
    rows = 2 * SUBLANE
    s = jnp.zeros((rows, D_MODEL), F32).at[:bsz].set(c).at[bsz].set(c_ctx)
    n_out = N_MOD * D_MODEL
    tn = D_MODEL
    out = pl.pallas_call(
        _adaln_kernel,
        grid=(DEPTH, n_out // tn),
        in_specs=[
            pl.BlockSpec((rows, D_MODEL), lambda i, j: (0, 0)),
            pl.BlockSpec((1, D_MODEL, tn), lambda i, j: (i, 0, j)),
            pl.BlockSpec((1, 1, tn), lambda i, j: (i, 0, j)),
        ],
        out_specs=pl.BlockSpec((1, rows, tn), lambda i, j: (i, 0, j)),
        out_shape=jax.ShapeDtypeStruct((DEPTH, rows, n_out), F32),
        compiler_params=_cparams("arbitrary", "arbitrary"),
        name="adaln",
    )(s, ada_w, ada_b.reshape(DEPTH, 1, n_out))
    return out.reshape(DEPTH, rows, N_MOD, D_MODEL)


def _token_call(body, lat_ins, ctx_ins, shared, out_widths, out_dtypes, *, mod, with_ctx, name,
                scratch_fn=None, n_halo=0, seg=None, joint_out=False):
    bsz = lat_ins[0].shape[0]
    if seg is None:
        seg = (lat_ins[0].shape[1], ctx_ins[0].shape[1] if with_ctx else 0)
    n_lat, n_ctx = seg
    tm = n_ctx if joint_out else TOKEN_ROWS
    nlb = n_lat // tm
    n_in, n_sh, n_out = len(lat_ins), len(shared), len(out_widths)
    sub_per_blk = tm // SUBLANE
    joint_in = [with_ctx and lat_ins[i] is ctx_ins[i] for i in range(n_in)]

    def lat_map(b, j):
        return (b, jnp.minimum(j, nlb - 1), 0)

    def ctx_map(b, j):
        return (b, 0, 0)

    def joint_ctx_map(b, j):
        return (b, n_lat // n_ctx, 0)

    def joint_out_map(b, j):
        return (b, j, 0)

    def prev_map(b, j):
        return (b, jnp.maximum(jnp.minimum(j, nlb - 1) * sub_per_blk - 1, 0), 0)

    def next_map(b, j):
        return (b, jnp.minimum((jnp.minimum(j, nlb - 1) + 1) * sub_per_blk, n_lat // SUBLANE - 1), 0)

    in_specs = [pl.BlockSpec((1, tm, a.shape[2]), lat_map) for a in lat_ins]
    args = list(lat_ins)
    for a in lat_ins[:n_halo]:
        in_specs += [pl.BlockSpec((1, SUBLANE, a.shape[2]), prev_map),
                     pl.BlockSpec((1, SUBLANE, a.shape[2]), next_map)]
        args += [a, a]
    if with_ctx:
        in_specs += [pl.BlockSpec((1, n_ctx, a.shape[2]), joint_ctx_map if jt else ctx_map)
                     for a, jt in zip(ctx_ins, joint_in)]
        args += list(ctx_ins)
    n_mod = 0
    if mod is not None:
        in_specs.append(pl.BlockSpec((1, N_MOD, D_MODEL), lambda b, j: (b, 0, 0)))
        args.append(mod)
        n_mod = 1
        if with_ctx:
            in_specs.append(pl.BlockSpec((1, N_MOD, D_MODEL), lambda b, j: (bsz, 0, 0)))
            args.append(mod)
            n_mod = 2
    in_specs += [_full_spec(s) for s in shared]
    args += list(shared)

    if joint_out:
        out_specs = [pl.BlockSpec((1, tm, w), joint_out_map) for w in out_widths]
        out_shape = [jax.ShapeDtypeStruct((bsz, n_lat + n_ctx, w), dt) for w, dt in zip(out_widths, out_dtypes)]
    else:
        out_specs = [pl.BlockSpec((1, tm, w), lat_map) for w in out_widths]
        out_shape = [jax.ShapeDtypeStruct((bsz, n_lat, w), dt) for w, dt in zip(out_widths, out_dtypes)]
        if with_ctx:
            out_specs += [pl.BlockSpec((1, n_ctx, w), ctx_map) for w in out_widths]
            out_shape += [jax.ShapeDtypeStruct((bsz, n_ctx, w), dt) for w, dt in zip(out_widths, out_dtypes)]
    scratch = scratch_fn(tm) if scratch_fn is not None else []

    def kern(*refs):
        pos = 0
        lat_r = refs[pos:pos + n_in]; pos += n_in
        halo_r = refs[pos:pos + 2 * n_halo]; pos += 2 * n_halo
        ctx_r = ()
        if with_ctx:
            ctx_r = refs[pos:pos + n_in]; pos += n_in
        mod_r = refs[pos:pos + n_mod]; pos += n_mod
        sh_r = refs[pos:pos + n_sh]; pos += n_sh
        lat_o = refs[pos:pos + n_out]; pos += n_out
        ctx_o = lat_o
        if with_ctx and not joint_out:
            ctx_o = refs[pos:pos + n_out]; pos += n_out
        scr = refs[pos:]
        j = pl.program_id(1)

        def run(ins, modr, outs, halo):
            extra = (halo,) if n_halo else ()
            vals = body(ins, modr, sh_r, scr, *extra)
            for o, v in zip(outs, vals):
                o[0] = v.astype(o.dtype)

        lat_halo = (halo_r[0::2], halo_r[1::2], j > 0, j < nlb - 1) if n_halo else None
        if with_ctx:

            @pl.when(j < nlb)
            def _():
                run(lat_r, mod_r[0] if n_mod else None, lat_o, lat_halo)

            @pl.when(j == nlb)
            def _():
                run(ctx_r, mod_r[1] if n_mod else None, ctx_o, None)
        else:
            run(lat_r, mod_r[0] if n_mod else None, lat_o, lat_halo)

    outs = pl.pallas_call(
        kern,
        grid=(bsz, nlb + (1 if with_ctx else 0)),
        in_specs=in_specs,
        out_specs=out_specs,
        out_shape=out_shape,
        scratch_shapes=scratch,
        compiler_params=_cparams("arbitrary", "arbitrary"),
        name=name,
    )(*args)
    if joint_out:
        return list(outs)
    lat_out = list(outs[:n_out])
    ctx_out = list(outs[n_out:]) if with_ctx else [None] * n_out
    return lat_out, ctx_out


def _mod_rows(mod_ref, i):
    return mod_ref[0, i:i + 1, :], mod_ref[0, i + 1:i + 2, :], mod_ref[0, i + 2:i + 3, :]


def _ffn_body(ins, mod_ref, sh, scr, *, si, n_mix=0, gi=None):
    h = ins[0][0]
    if n_mix:
        gate = mod_ref[0, gi:gi + 1, :]
        y = _dot(ins[1][0], sh[0][...])
        for k in range(1, n_mix):
            y = y + _dot(ins[1 + k][0], sh[k][...])
        h = _layer_norm(ALPHA * h + gate * y, sh[n_mix][...], sh[n_mix + 1][...])
        sh = sh[n_mix + 2:]
    w_in, w_out, g, b = sh
    act = scr[0]
    shift, scale, gate = _mod_rows(mod_ref, si)
    rows = h.shape[0]
    xb = (h * (1.0 + scale) + shift).astype(BF16)
    for j in range(D_FF // FF_CHUNK):
        lo = j * FF_CHUNK
        a = _dot(xb, w_in[:, lo:lo + FF_CHUNK])
        bb = _dot(xb, w_in[:, D_FF + lo:D_FF + lo + FF_CHUNK])
        act[0:rows, lo:lo + FF_CHUNK] = (a * jax.nn.sigmoid(a) * bb).astype(BF16)
    y = _dot(act[0:rows, :], w_out[...])
    return [_layer_norm(ALPHA * h + gate * (MACARON * y), g[...], b[...])]


def _ffn(h_lat, h_ctx, mod, si, w_in, w_out, g, b, with_ctx=True, mix=None):
    row = lambda t: t.reshape(1, -1)
    lat_ins, ctx_ins, shared, kw = [h_lat], [h_ctx], [], {}
    if mix is not None:
        ms_lat, ms_ctx, ws, gi, g_mix, b_mix = mix
        lat_ins, ctx_ins = lat_ins + list(ms_lat), ctx_ins + list(ms_ctx)
        shared = list(ws) + [row(g_mix), row(b_mix)]
        kw = dict(n_mix=len(ws), gi=gi)
    seg = (h_lat.shape[1], h_ctx.shape[1] if with_ctx else 0)
    lat, ctx = _token_call(
        functools.partial(_ffn_body, si=si, **kw), lat_ins, ctx_ins, shared + [w_in, w_out, row(g), row(b)],
        [D_MODEL], [F32], mod=mod, with_ctx=with_ctx, name="ffn_half_step", seg=seg,
        scratch_fn=lambda tm: [pltpu.VMEM((tm, D_FF), BF16)])
    return lat[0], ctx[0]


def _proj_body(ins, mod_ref, sh, scr, *, si):
    h = ins[0][0]
    shift, scale, _ = _mod_rows(mod_ref, si)
    xb = (h * (1.0 + scale) + shift).astype(BF16)
    return [_dot(xb, w[...]) for w in sh]


def _proj(h_lat, h_ctx, mod, si, ws, joint=False):
    body = functools.partial(_proj_body, si=si)
    return _token_call(body, [h_lat], [h_ctx], ws, [w.shape[1] for w in ws], [F32] * len(ws),
                       mod=mod, with_ctx=True, name="mixer_in_proj", joint_out=joint)


def _chunk_call_all(body, ins, shared, out_widths, out_dtypes, *, seg, chunk, rev, scratch, name,
                    per_batch=False):
    n_lat, n_ctx = seg
    bsz = ins[0][0].shape[0]
    ncl, ncc = n_lat // chunk, n_ctx // chunk
    n_all = ncl + ncc
    rows = 1 if per_batch else bsz

    def blk(n):
        if rev:
            return n_all - 1 - n
        return jnp.where(n < ncc, ncl + n, n - ncc)

    def index(cb):
        if per_batch:
            return lambda b, n: (b, blk(n), cb)
        return lambda n: (0, blk(n), cb)

    in_specs = [pl.BlockSpec((rows, chunk, w), index(cb)) for _, w, cb in ins]
    in_specs += [_full_spec(s) for s in shared]
    out_specs = [pl.BlockSpec((rows, chunk, w), index(0)) for w in out_widths]
    out_shape = [jax.ShapeDtypeStruct((bsz, n_lat + n_ctx, w), dt) for w, dt in zip(out_widths, out_dtypes)]
    n_i, n_s, n_o = len(ins), len(shared), len(out_widths)

    def kern(*refs):
        in_r, sh = refs[:n_i], refs[n_i:n_i + n_s]
        out_r = refs[n_i + n_s:n_i + n_s + n_o]
        scr = refs[n_i + n_s + n_o:]
        outs = body(pl.program_id(1 if per_batch else 0), [r[...] for r in in_r], sh, scr)
        for o, v in zip(out_r, outs):
            o[...] = v.astype(o.dtype)

    return pl.pallas_call(
        kern,
        grid=(bsz, n_all) if per_batch else (n_all,),
        in_specs=in_specs,
        out_specs=out_specs,
        out_shape=out_shape,
        scratch_shapes=scratch,
        compiler_params=_cparams(*(["arbitrary"] * (2 if per_batch else 1))),
        name=name,
    )(*[a for a, _, _ in ins], *shared)


S5_WIDTH = 256
S5_GROUP = 16
S5_GROUPS = 16
S5_STATE = 64
S5_T = 4
S5_ROW = S5_T * S5_WIDTH
S5_NSTATE = S5_GROUPS * S5_STATE


def _s5_operators(lam_re, lam_im, log_dt, b_re, b_im, c_re, c_im):
    t_len, g_n, p_n, c_n = S5_T, S5_GROUPS, S5_STATE, S5_GROUP
    dt = jnp.exp(log_dt)[..., None]
    a_r, a_i = lam_re * dt, lam_im * dt
    mag = jnp.exp(a_r)
    lb_re, lb_im = mag * jnp.cos(a_i), mag * jnp.sin(a_i)
    den = lam_re * lam_re + lam_im * lam_im
    nr = lb_re - 1.0
    coef_re = (nr * lam_re + lb_im * lam_im) / den
    coef_im = (lb_im * lam_re - nr * lam_im) / den
    bb_re = coef_re[..., None] * b_re - coef_im[..., None] * b_im
    bb_im = coef_re[..., None] * b_im + coef_im[..., None] * b_re

    def lam_pow(e):
        m = jnp.exp(a_r * e)
        return m * jnp.cos(a_i * e), m * jnp.sin(a_i * e)

    eye = jnp.eye(g_n, dtype=F32)

    def kern_tau(d, tau):
        pr, pi = lam_pow(float(tau))
        cp_re = c_re * pr[d][:, None, :] - c_im * pi[d][:, None, :]
        cp_im = c_re * pi[d][:, None, :] + c_im * pr[d][:, None, :]
        k = (jnp.einsum("gcp,gpe->gec", cp_re, bb_re[d], precision=HI)
             - jnp.einsum("gcp,gpe->gec", cp_im, bb_im[d], precision=HI))
        return k

    k0 = [kern_tau(0, tau) for tau in range(t_len)]
    k1 = [kern_tau(1, tau) for tau in range(t_len)]
    zero = jnp.zeros_like(k0[0])
    rows = []
    for s in range(t_len):
        cols = []
        for t in range(t_len):
            blk = zero
            if s <= t:
                blk = blk + k0[t - s]
            if s >= t:
                blk = blk + k1[s - t]
            cols.append(blk)
        rows.append(jnp.stack(cols, 0))
    kst = jnp.stack(rows, 0)
    m_op = jnp.einsum("stgec,gh->sgethc", kst, eye).reshape(S5_ROW, S5_ROW)

    w_ops, v_ops, lam_t = [], [], []
    for d in range(2):
        w_re, w_im, v_re, v_im = [], [], [], []
        for s in range(t_len):
            pr, pi = lam_pow(float(t_len - 1 - s) if d == 0 else float(s))
            w_re.append(pr[d][..., None] * bb_re[d] - pi[d][..., None] * bb_im[d])
            w_im.append(pr[d][..., None] * bb_im[d] + pi[d][..., None] * bb_re[d])
        for t in range(t_len):
            pr, pi = lam_pow(float(t + 1) if d == 0 else float(t_len - t))
            v_re.append(c_re * pr[d][:, None, :] - c_im * pi[d][:, None, :])
            v_im.append(-(c_re * pi[d][:, None, :] + c_im * pr[d][:, None, :]))
        for w in (w_re, w_im):
            w_ops.append(jnp.einsum("sgpe,gh->sgehp", jnp.stack(w, 0), eye).reshape(S5_ROW, S5_NSTATE))
        for v in (v_re, v_im):
            v_ops.append(jnp.einsum("tgcp,gh->gpthc", jnp.stack(v, 0), eye).reshape(S5_NSTATE, S5_ROW))
        pr, pi = lam_pow(float(t_len))
        lam_t.append((pr[d].reshape(1, S5_NSTATE), pi[d].reshape(1, S5_NSTATE)))
    w_cat = jnp.concatenate([m_op] + w_ops, axis=1).astype(BF16)
    return w_cat, [v.astype(BF16) for v in v_ops], lam_t


def _s5_drive_body(ins, mod_ref, sh, scr):
    xb = ins[0][0].astype(BF16)
    w = sh[0]
    n = w.shape[1] // S5_ROW
    return [_dot(xb, w[:, i * S5_ROW:(i + 1) * S5_ROW]) for i in range(n)]


def _s5_state_scan(b_re, b_im, lam, rev):
    (bre_l, bre_c), (bim_l, bim_c) = b_re, b_im
    bsz, nl, w = bre_l.shape
    nc = bre_c.shape[1]

    def flat(a):
        return a.reshape(bsz * a.shape[1], w)

    def kern(brl, bil, brc, bic, lr_ref, li_ref, hrl, hil, hrc, hic):
        lr = jnp.broadcast_to(lr_ref[...], (bsz, LANE))
        li = jnp.broadcast_to(li_ref[...], (bsz, LANE))

        def make(br, bi, hr_o, hi_o, nrows):
            def step(i, carry):
                hr, hi = carry
                n = (nrows - 1 - i) if rev else i
                idx = pl.ds(n, bsz, stride=nrows)
                hr_o[idx, :] = hr
                hi_o[idx, :] = hi
                xr, xi = br[idx, :], bi[idx, :]
                return lr * hr - li * hi + xr, lr * hi + li * hr + xi
            return step

        z = jnp.zeros((bsz, LANE), F32)
        carry = lax.fori_loop(0, nc, make(brc, bic, hrc, hic, nc), (z, z), unroll=8)
        lax.fori_loop(0, nl, make(brl, bil, hrl, hil, nl), carry, unroll=8)

    def spec(rows):
        return pl.BlockSpec((rows, LANE), lambda j: (0, j))

    lam_spec = pl.BlockSpec((1, LANE), lambda j: (0, j))
    outs = pl.pallas_call(
        kern,
        grid=(w // LANE,),
        in_specs=[spec(bsz * nl), spec(bsz * nl), spec(bsz * nc), spec(bsz * nc), lam_spec, lam_spec],
        out_specs=[spec(bsz * nl), spec(bsz * nl), spec(bsz * nc), spec(bsz * nc)],
        out_shape=[jax.ShapeDtypeStruct((bsz * nl, w), F32), jax.ShapeDtypeStruct((bsz * nl, w), F32),
                   jax.ShapeDtypeStruct((bsz * nc, w), F32), jax.ShapeDtypeStruct((bsz * nc, w), F32)],
        compiler_params=_cparams("arbitrary"),
        name="s5_state_scan",
    )(flat(bre_l), flat(bim_l), flat(bre_c), flat(bim_c), lam[0], lam[1])
    hrl, hil, hrc, hic = outs
    return (hrl.reshape(bsz, nl, w), hrc.reshape(bsz, nc, w)), (hil.reshape(bsz, nl, w), hic.reshape(bsz, nc, w))


def _s5_finish_body(ins, mod_ref, sh, scr):
    yi, h0r, h0i, h1r, h1i, u = [r[0] for r in ins]
    v0r, v0i, v1r, v1i, d_skip, glu_w, glu_b = sh
    y = yi + d_skip[...] * u
    for h, v in ((h0r, v0r), (h0i, v0i), (h1r, v1r), (h1i, v1i)):
        y = y + _dot(h.astype(BF16), v[...])
    z = jax.nn.gelu(y)
    return [z * jax.nn.sigmoid(_dot(z.astype(BF16), glu_w[...]) + glu_b[...])]


def _s5_branch(u_lat, u_ctx, ops, d_skip, glu_w, glu_b):
    w_cat, v_ops, lam_t = ops
    bsz, n_lat, _ = u_lat.shape
    n_ctx = u_ctx.shape[1]
    u4l = u_lat.reshape(bsz, n_lat // S5_T, S5_ROW)
    u4c = u_ctx.reshape(bsz, n_ctx // S5_T, S5_ROW)
    lat, ctx = _token_call(_s5_drive_body, [u4l], [u4c], [w_cat], [S5_ROW] * 5, [F32] * 5,
                           mod=None, with_ctx=True, name="s5_drive")
    pairs = list(zip(lat, ctx))
    h0 = _s5_state_scan(pairs[1], pairs[2], lam_t[0], rev=False)
    h1 = _s5_state_scan(pairs[3], pairs[4], lam_t[1], rev=True)
    eye_t = jnp.eye(S5_T, dtype=F32)
    glu_k = jnp.kron(eye_t, glu_w).astype(BF16)
    fin_l = [pairs[0][0], h0[0][0], h0[1][0], h1[0][0], h1[1][0], u4l]
    fin_c = [pairs[0][1], h0[0][1], h0[1][1], h1[0][1], h1[1][1], u4c]
    shared = list(v_ops) + [jnp.tile(d_skip, S5_T).reshape(1, S5_ROW), glu_k, jnp.tile(glu_b, S5_T).reshape(1, S5_ROW)]
    lat, ctx = _token_call(_s5_finish_body, fin_l, fin_c, shared, [S5_ROW], [BF16],
                           mod=None, with_ctx=True, name="s5_finish")
    return lat[0].reshape(bsz, n_lat, S5_WIDTH), ctx[0].reshape(bsz, n_ctx, S5_WIDTH)


RET_HEADS = 6
RET_HEAD_DIM = 128
RET_WIDTH = RET_HEADS * RET_HEAD_DIM
RET_CHUNK = 256
ROPE_BASE = 10000.0


def _rope_tables(n_lat):
    nf = RET_HEAD_DIM // 4
    rows = n_lat // GRID_W
    freqs = ROPE_BASE ** (-jnp.arange(nf, dtype=F32) / nf)
    pr = jnp.broadcast_to(jnp.arange(rows, dtype=F32)[:, None], (rows, GRID_W)).reshape(-1)
    pc = jnp.broadcast_to(jnp.arange(GRID_W, dtype=F32)[None, :], (rows, GRID_W)).reshape(-1)
    ang = jnp.concatenate([pr[:, None] * freqs, pc[:, None] * freqs], -1)
    cos, sin = jnp.cos(ang), jnp.sin(ang)
    return jnp.concatenate([cos, cos], -1), jnp.concatenate([-sin, sin], -1)


def _ret_tables(log_rate):
    c = RET_CHUNK
    lg = -jnp.exp(log_rate.astype(F32))
    idx = jnp.arange(c, dtype=F32)
    diff = idx[:, None] - idx[None, :]
    past, fut = diff >= 0, diff <= 0
    d0 = jnp.where(past, jnp.exp(jnp.where(past, diff, 0.0)[None] * lg[0][:, None, None]), 0.0)
    d1 = jnp.where(fut, jnp.exp(jnp.where(fut, -diff, 0.0)[None] * lg[1][:, None, None]), 0.0)

    def rep(t):
        return jnp.repeat(t, RET_HEAD_DIM, axis=-1)

    fwd = (rep(jnp.exp((idx + 1.0)[:, None] * lg[0])), rep(jnp.exp((c - 1.0 - idx)[:, None] * lg[0])),
           rep(jnp.exp(c * lg[0])[None]))
    bwd = (rep(jnp.exp((c - idx)[:, None] * lg[1])), rep(jnp.exp(idx[:, None] * lg[1])),
           rep(jnp.exp(c * lg[1])[None]))
    return d0 + d1, fwd, bwd


def _rope(x, cos, sin):
    return x * cos + pltpu.roll(x, RET_HEAD_DIM // 2, 1) * sin


def _proj_rope_body(ins, mod_ref, sh, scr, *, si):
    h, cos, sin = ins[0][0], ins[1][0], ins[2][0]
    shift, scale, _ = _mod_rows(mod_ref, si)
    xb = (h * (1.0 + scale) + shift).astype(BF16)

    def rotated(w, mult):
        y = _dot(xb, w[...])
        cols = [_rope(y[:, i * RET_HEAD_DIM:(i + 1) * RET_HEAD_DIM], cos, sin) * mult for i in range(RET_HEADS)]
        return jnp.concatenate(cols, -1)

    return [rotated(sh[0], 1.0), rotated(sh[1], RET_HEAD_DIM ** -0.5)] + [_dot(xb, w[...]) for w in sh[2:]]


def _ret_bwd_body(n, vals, sh, scr):
    q, k, v = (t[0] for t in vals)
    xi, zeta, gc = sh
    state = scr[0]

    @pl.when(n == 0)
    def _():
        state[...] = jnp.zeros_like(state)

    outs = []
    for h in range(RET_HEADS):
        sl = slice(h * RET_HEAD_DIM, (h + 1) * RET_HEAD_DIM)
        s_old = state[h]
        outs.append(_dot((q[:, sl] * xi[:, sl]).astype(BF16), s_old.astype(BF16)))
        state[h] = gc[:, sl] * s_old + _dot_tn((k[:, sl] * zeta[:, sl]).astype(BF16), v[:, sl].astype(BF16))
    return [jnp.concatenate(outs, -1)[None]]


def _ret_fwd_body(n, vals, sh, scr):
    q, k, v, g, o_bwd = (t[0] for t in vals)
    dcomb, xi, zeta, gc = sh
    state = scr[0]

    @pl.when(n == 0)
    def _():
        state[...] = jnp.zeros_like(state)

    outs = []
    for h in range(RET_HEADS):
        sl = slice(h * RET_HEAD_DIM, (h + 1) * RET_HEAD_DIM)
        qh, kh = q[:, sl], k[:, sl]
        vb = v[:, sl].astype(BF16)
        s_old = state[h]
        scores = _dot_nt(qh.astype(BF16), kh.astype(BF16)) * dcomb[h]
        o = (_dot(scores.astype(BF16), vb) + _dot((qh * xi[:, sl]).astype(BF16), s_old.astype(BF16))
             + o_bwd[:, sl])
        state[h] = gc[:, sl] * s_old + _dot_tn((kh * zeta[:, sl]).astype(BF16), vb)
        mu = jnp.mean(o, -1, keepdims=True)
        oc = o - mu
        var = jnp.mean(oc * oc, -1, keepdims=True)
        gh = g[:, sl]
        outs.append(gh * jax.nn.sigmoid(gh) * (oc * lax.rsqrt(var + LN_EPS)))
    return [jnp.concatenate(outs, -1)[None]]


def _retention_branch(q, k, vg, seg, log_rate):
    dcomb, fwd, bwd = _ret_tables(log_rate)
    w = RET_WIDTH
    state = [pltpu.VMEM((RET_HEADS, RET_HEAD_DIM, RET_HEAD_DIM), F32)]
    qkv = [(q, w, 0), (k, w, 0), (vg, w, 0)]
    (o_bwd,) = _chunk_call_all(_ret_bwd_body, qkv, list(bwd), [w], [F32], seg=seg, chunk=RET_CHUNK,
                               rev=True, scratch=state, name="retention_bwd", per_batch=True)
    ins = qkv + [(vg, w, 1), (o_bwd, w, 0)]
    (r,) = _chunk_call_all(_ret_fwd_body, ins, [dcomb] + list(fwd), [w], [BF16], seg=seg, chunk=RET_CHUNK,
                           rev=False, scratch=state, name="retention_fwd", per_batch=True)
    return r


def _mixer_ab(h_lat, h_ctx, mod, w_in, w_out, s5_ops, s5_d, s5_glu_w, s5_glu_b, ret_log_rate):
    bsz, n_lat, _ = h_lat.shape
    n_ctx = h_ctx.shape[1]
    w = RET_WIDTH
    lo_q, lo_k, lo_v = S5_WIDTH, S5_WIDTH + w, S5_WIDTH + 2 * w
    ws = [w_in[:, lo_q:lo_k], w_in[:, lo_k:lo_v], w_in[:, lo_v:], w_in[:, :S5_WIDTH]]
    cos, sin = _rope_tables(n_lat)
    full = lambda t, n: jnp.broadcast_to(t[None], (bsz, n, RET_HEAD_DIM))
    lat_ins = [h_lat, full(cos, n_lat), full(sin, n_lat)]
    ctx_ins = [h_ctx, jnp.ones((bsz, n_ctx, RET_HEAD_DIM), F32), jnp.zeros((bsz, n_ctx, RET_HEAD_DIM), F32)]
    q, k, vg, u = _token_call(functools.partial(_proj_rope_body, si=3), lat_ins, ctx_ins,
                              [t.astype(BF16) for t in ws], [w, w, 2 * w, S5_WIDTH], [F32] * 4,
                              mod=mod, with_ctx=True, name="mixer_in_proj", joint_out=True)
    a_lat, a_ctx = _s5_branch(u[:, :n_lat], u[:, n_lat:], s5_ops, s5_d, s5_glu_w, s5_glu_b)
    r = _retention_branch(q, k, vg, (n_lat, n_ctx), ret_log_rate)
    return [a_lat, r], [a_ctx, r], [w_out[:S5_WIDTH].astype(BF16), w_out[S5_WIDTH:].astype(BF16)]


def _mm(a, b, exact=False):
    if exact:
        return jnp.dot(a, b, preferred_element_type=F32, precision=HI)
    return _dot(a.astype(BF16), b.astype(BF16))


def _softplus(x):
    return jnp.maximum(x, 0.0) + jnp.log1p(jnp.exp(-jnp.abs(x)))


def _shift_rows(x, s, prev8, next8):
    rows = x.shape[0]
    rolled = pltpu.roll(x, (-s) % rows, 0)
    r8 = lax.broadcasted_iota(jnp.int32, (SUBLANE, x.shape[1]), 0)
    if s < 0:
        edge = jnp.where(r8 < -s, pltpu.roll(prev8, -s, 0), rolled[:SUBLANE])
        return jnp.concatenate([edge, rolled[SUBLANE:]], 0)
    edge = jnp.where(r8 >= SUBLANE - s, pltpu.roll(next8, SUBLANE - s, 0), rolled[rows - SUBLANE:])
    return jnp.concatenate([rolled[:rows - SUBLANE], edge], 0)


def _halo_rows(halo, i, width):
    if halo is None:
        z = jnp.zeros((SUBLANE, width), F32)
        return z, z
    prev_r, next_r, has_prev, has_next = halo
    return (jnp.where(has_prev, prev_r[i][0], 0.0), jnp.where(has_next, next_r[i][0], 0.0))


def _order_masks(c, rev, reps=1):
    ri = lax.broadcasted_iota(jnp.int32, (c, reps * c), 0)
    ci = jnp.bitwise_and(lax.broadcasted_iota(jnp.int32, (c, reps * c), 1), c - 1)
    if rev:
        return ri <= ci, ri < ci
    return ri >= ci, ri > ci


TRI_BASE = 8


def _bmm(a, b):
    return lax.dot_general(a.astype(BF16), b.astype(BF16), (((2,), (1,)), ((0,), (0,))),
                           preferred_element_type=F32)


def _bmm_nt(a, b):
    return lax.dot_general(a.astype(BF16), b.astype(BF16), (((2,), (2,)), ((0,), (0,))),
                           preferred_element_type=F32)


def _bmm_tn(a, b):
    return lax.dot_general(a.astype(BF16), b.astype(BF16), (((1,), (1,)), ((0,), (0,))),
                           preferred_element_type=F32)


def _tri_inverse(n_mat, mul):
    c, lanes = n_mat.shape[1], n_mat.shape[2]
    ri = lax.broadcasted_iota(jnp.int32, (c, lanes), 0)
    ci = jnp.bitwise_and(lax.broadcasted_iota(jnp.int32, (c, lanes), 1), c - 1)

    def same_block(bits):
        return jnp.right_shift(ri, bits) == jnp.right_shift(ci, bits)

    bits = TRI_BASE.bit_length() - 1
    same = same_block(bits)
    p = jnp.where(same, n_mat, 0.0)
    x = jnp.where(ri == ci, 1.0, 0.0) + p
    k = 2
    while k < TRI_BASE:
        p = mul(p, p)
        x = x + mul(x, p)
        k *= 2
    while (1 << bits) < c:
        bits += 1
        same2 = same_block(bits)
        off = jnp.where(jnp.logical_and(same2, jnp.logical_not(same)), n_mat, 0.0)
        x = x + mul(mul(x, off), x)
        same = same2
    return x


def _heads_to_batch(x, n):
    w = x.shape[2] // n
    return jnp.concatenate([x[:, :, i * w:(i + 1) * w] for i in range(n)], axis=0)


def _batch_to_heads(x, n):
    b = x.shape[0] // n
    return jnp.concatenate([x[i * b:(i + 1) * b] for i in range(n)], axis=-1)


def _scan_cumsum(tri, x):
    bsz, _, w = x.shape
    flat = jnp.concatenate([x[b] for b in range(bsz)], axis=-1)
    cum = _mm(tri, flat, exact=True)
    return jnp.stack([cum[:, b * w:(b + 1) * w] for b in range(bsz)], axis=0)


def _seg_sum(x, seg):
    hi = x.astype(BF16)
    lo = (x - hi.astype(F32)).astype(BF16)
    return _dot(hi, seg) + _dot(lo, seg)


GDN_HEADS = 4
GDN_HEAD_DIM = 128
GDN_WIDTH = GDN_HEADS * GDN_HEAD_DIM
GDN_CONV = 5
GDN_CHUNK = 64


def _gdn_prep_body(ins, mod_ref, sh, scr, halo):
    qkv, sm = ins[0][0], ins[1][0]
    conv_w, a_neg, dt_bias = sh
    prev8, next8 = _halo_rows(halo, 0, qkv.shape[1])
    half = GDN_CONV // 2
    acc = conv_w[half:half + 1, :] * qkv
    for i in range(GDN_CONV):
        if i != half:
            acc = acc + conv_w[i:i + 1, :] * _shift_rows(qkv, i - half, prev8, next8)
    y = acc * jax.nn.sigmoid(acc)
    outs = []
    for part, scale in ((0, GDN_HEAD_DIM ** -0.5), (1, 1.0)):
        cols = []
        for h in range(GDN_HEADS):
            lo = part * GDN_WIDTH + h * GDN_HEAD_DIM
            t = y[:, lo:lo + GDN_HEAD_DIM]
            cols.append(t * (lax.rsqrt(jnp.sum(t * t, -1, keepdims=True) + 1e-6) * scale))
        outs.append(jnp.concatenate(cols, -1))
    outs.append(y[:, 2 * GDN_WIDTH:])
    lane = lax.broadcasted_iota(jnp.int32, sm.shape, 1)
    g = a_neg[...] * _softplus(sm + dt_bias[...])
    outs.append(jnp.where(lane < 2 * GDN_HEADS, g, jax.nn.sigmoid(sm)))
    return outs


def _gdn_body(n, vals, sh, scr, *, rev, last):
    q, k, v, sm = vals[:4]
    state = scr[0]
    c = GDN_CHUNK
    bsz = q.shape[0]

    @pl.when(n == 0)
    def _():
        state[...] = jnp.zeros_like(state)

    later, strict = _order_masks(c, rev)
    gcum = _scan_cumsum(later.astype(F32), sm)
    gcum_t = [gcum[b].T for b in range(bsz)]
    end = 0 if rev else c - 1
    d = 1 if rev else 0
    g_col, g_row, beta = [], [], []
    for h in range(GDN_HEADS):
        col = d * GDN_HEADS + h
        bcol = 2 * GDN_HEADS + col
        for b in range(bsz):
            g_col.append(gcum[b][:, col:col + 1])
            g_row.append(gcum_t[b][col:col + 1, :])
            beta.append(sm[b][:, bcol:bcol + 1])
    gc, gr, beta = jnp.stack(g_col), jnp.stack(g_row), jnp.stack(beta)
    dec = jnp.where(later, jnp.exp(jnp.where(later, gc - gr, 0.0)), 0.0)
    qg, kg, vg = (_heads_to_batch(t, GDN_HEADS) for t in (q, k, v))
    kb = kg * beta
    scores = _bmm_nt(jnp.concatenate([kb, qg], axis=1), kg)
    a_mat = jnp.where(strict, scores[:, :c] * dec, 0.0)
    attn = scores[:, c:] * dec
    t_mat = _tri_inverse(-a_mat, _bmm)
    eg = jnp.exp(gc)
    wu = _bmm(t_mat, jnp.concatenate([kb * eg, vg * beta], axis=-1))
    w, u = wu[:, :, :GDN_HEAD_DIM], wu[:, :, GDN_HEAD_DIM:]
    g_end = gc[:, end:end + 1, :]
    s_old = state[...]
    ws = _bmm(jnp.concatenate([w, qg * eg], axis=1), s_old)
    v_new = u - ws[:, :c]
    o = ws[:, c:] + _bmm(attn, v_new)
    state[...] = s_old * jnp.exp(g_end) + _bmm_tn(kg * jnp.exp(g_end - gc), v_new)
    o = _batch_to_heads(o, GDN_HEADS)
    if not last:
        return [o]
    z, o_prev = vals[4], vals[5]
    norm_w = sh[0]
    o = o + o_prev
    cols = []
    for h in range(GDN_HEADS):
        t = o[:, :, h * GDN_HEAD_DIM:(h + 1) * GDN_HEAD_DIM]
        cols.append(t * lax.rsqrt(jnp.mean(t * t, -1, keepdims=True) + 1e-6) * norm_w[...])
    return [jnp.concatenate(cols, -1) * (z * jax.nn.sigmoid(z))]


def _deltanet_branch(qkv, z, sm, seg, conv_w, a_log, dt_bias, norm_w):
    bsz = qkv.shape[0]
    pad = sm.shape[2] - 2 * GDN_HEADS
    a_neg = jnp.pad(-jnp.exp(a_log.astype(F32)).reshape(1, -1), ((0, 0), (0, pad)))
    dtb = jnp.pad(dt_bias.astype(F32).reshape(1, -1), ((0, 0), (0, pad)))
    w = GDN_WIDTH
    prep = _token_call(_gdn_prep_body, [qkv, sm], [qkv, sm], [conv_w, a_neg, dtb],
                       [w, w, w, sm.shape[2]], [F32] * 4, mod=None, with_ctx=True,
                       name="gdn_prep", n_halo=1, seg=seg, joint_out=True)
    ins = [(a, a.shape[2], 0) for a in prep]
    state = [pltpu.VMEM((GDN_HEADS * bsz, GDN_HEAD_DIM, GDN_HEAD_DIM), F32)]
    (o_bwd,) = _chunk_call_all(functools.partial(_gdn_body, rev=True, last=False), ins, [], [w], [F32],
                               seg=seg, chunk=GDN_CHUNK, rev=True, scratch=state, name="gdn_bwd")
    ins = ins + [(z, w, 0), (o_bwd, w, 0)]
    (out,) = _chunk_call_all(functools.partial(_gdn_body, rev=False, last=True), ins, [norm_w.reshape(1, 1, -1)],
                             [w], [BF16], seg=seg, chunk=GDN_CHUNK, rev=False, scratch=state, name="gdn_fwd")
    return out


RWKV_HEADS = 8
RWKV_HEAD_DIM = 64
RWKV_WIDTH = RWKV_HEADS * RWKV_HEAD_DIM
RWKV_DECAY_LORA = 32
RWKV_AAA_LORA = 32
RWKV_GATE_LORA = 96
RWKV_GN_EPS = 64e-5
RWKV_CHUNK = 64
RWKV_LOG_DECAY_SCALE = -math.exp(-0.5)
RWKV_PAIRS = RWKV_WIDTH // LANE


def _rwkv_prep_body(ins, mod_ref, sh, scr, halo):
    rkv, sm = ins[0][0], ins[1][0]
    mu_main, mu_sm, k_k, k_a, r_k, w0, a0, lora_w, seg = sh
    w = RWKV_WIDTH

    def lerp(x, mu, i):
        prev8, next8 = _halo_rows(halo, i, x.shape[1])
        xs = 0.5 * (_shift_rows(x, -1, prev8, next8) + _shift_rows(x, 1, prev8, next8))
        return x + mu[...] * (xs - x)

    rkv = lerp(rkv, mu_main, 0)
    sm = lerp(sm, mu_sm, 1)
    r, k, v = rkv[:, :w], rkv[:, w:2 * w], rkv[:, 2 * w:]
    lane = lax.broadcasted_iota(jnp.int32, sm.shape, 1)
    n_dec, n_aaa = 2 * RWKV_DECAY_LORA, 2 * RWKV_AAA_LORA
    t = jnp.where(lane < n_dec, jnp.tanh(sm), jnp.where(lane < n_dec + n_aaa, sm, jax.nn.sigmoid(sm)))
    lo = _dot(t.astype(BF16), lora_w[...])
    kk = k * k_k[...]
    kk = kk * lax.rsqrt(_seg_sum(kk * kk, seg[...]) + 1e-6)
    outs = [r, v, kk]
    k_sum = None
    for d in range(2):
        log_w = RWKV_LOG_DECAY_SCALE * jax.nn.sigmoid(w0[d:d + 1, :] + lo[:, d * w:(d + 1) * w])
        a = jax.nn.sigmoid(a0[d:d + 1, :] + lo[:, (2 + d) * w:(3 + d) * w])
        k_d = k * (1.0 + (a - 1.0) * k_a[...])
        outs += [k_d, log_w, kk * a]
        k_sum = k_d if k_sum is None else k_sum + k_d
    outs.append(lo[:, 4 * w:])
    outs.append(_seg_sum(r * k_sum * r_k[...], seg[...]) * v)
    return outs


def _head_rows(x):
    first = lax.broadcasted_iota(jnp.int32, x.shape[1:], 1) < RWKV_HEAD_DIM
    return jnp.concatenate([jnp.where(first, x, 0.0), jnp.where(first, 0.0, x)], axis=1)


def _pair_mul(x, y):
    return _bmm(x, _head_rows(y))


def _rwkv_body(n, vals, sh, scr, *, rev, last):
    r, k, v, kk, lw, b = vals[:6]
    state = scr[0]
    c = RWKV_CHUNK
    bsz = r.shape[0]

    @pl.when(n == 0)
    def _():
        state[...] = jnp.zeros_like(state)

    later, _ = _order_masks(c, rev)
    later2, strict2 = _order_masks(c, rev, reps=2)
    cum = _scan_cumsum(later.astype(F32), lw)
    end = 0 if rev else c - 1
    cum_end = cum[:, end:end + 1, :]
    e_inv = jnp.exp(-cum)
    e_tail = jnp.exp(cum_end - cum)
    to_pairs = functools.partial(_heads_to_batch, n=RWKV_PAIRS)
    r_t = to_pairs(r * jnp.exp(cum))
    a_t = to_pairs(-kk * jnp.exp(cum - lw))
    b_t, k_t = to_pairs(b * e_inv), to_pairs(k * e_inv)
    b_h, k_h = to_pairs(b * e_tail), to_pairs(k * e_tail)
    p_end = to_pairs(jnp.exp(cum_end))
    v_p = to_pairs(v)
    prods = _bmm_nt(jnp.concatenate([a_t, r_t], axis=1),
                    jnp.concatenate([_head_rows(b_t), _head_rows(k_t)], axis=1))
    a_ab = jnp.where(strict2, prods[:, :c, :LANE], 0.0)
    a_ak = jnp.where(strict2, prods[:, :c, LANE:], 0.0)
    a_rb = jnp.where(later2, prods[:, c:, :LANE], 0.0)
    a_rk = jnp.where(later2, prods[:, c:, LANE:], 0.0)
    inv = _tri_inverse(a_ab, _pair_mul)
    akv = _pair_mul(a_ak, v_p)
    wu = _bmm(inv, jnp.concatenate([_head_rows(a_t), _head_rows(akv)], axis=-1))
    w_t, u_t = wu[:, :, :LANE], wu[:, :, LANE:]
    s_old = state[...]
    ws = _bmm_nt(jnp.concatenate([w_t, r_t], axis=1), s_old)
    u = ws[:, :c] + u_t
    y = ws[:, c:] + _bmm(jnp.concatenate([a_rb, a_rk], axis=-1),
                         jnp.concatenate([_head_rows(u), _head_rows(v_p)], axis=1))
    s_new = s_old * p_end + _bmm_tn(jnp.concatenate([u, v_p], axis=1), jnp.concatenate([b_h, k_h], axis=1))
    bi = lax.broadcasted_iota(jnp.int32, (LANE, LANE), 0) < RWKV_HEAD_DIM
    bj = lax.broadcasted_iota(jnp.int32, (LANE, LANE), 1) < RWKV_HEAD_DIM
    state[...] = jnp.where(bi == bj, s_new, 0.0)
    y = _batch_to_heads(y, RWKV_PAIRS)
    if not last:
        return [y]
    y_prev, g, bonus = vals[6:9]
    ln_w, ln_b, seg = sh
    w = RWKV_WIDTH
    y = (y + y_prev).reshape(bsz * c, w)
    inv_n = 1.0 / RWKV_HEAD_DIM
    mu = _seg_sum(y, seg[...]) * inv_n
    yc = y - mu
    var = _seg_sum(yc * yc, seg[...]) * inv_n
    out = (yc * lax.rsqrt(var + RWKV_GN_EPS)) * ln_w[...] + ln_b[...]
    return [(out.reshape(bsz, c, w) + bonus) * g]


def _rwkv7_branch(rkv, sm, seg_len, mu, w0, w_up, a0, a_up, g_up, k_k, k_a, r_k, ln_w, ln_b):
    w = RWKV_WIDTH
    bsz = rkv.shape[0]
    n_sm = sm.shape[2]
    n_used = 2 * RWKV_DECAY_LORA + 2 * RWKV_AAA_LORA + RWKV_GATE_LORA
    mu_main = mu[:3 * w].reshape(1, -1)
    mu_sm = jnp.pad(mu[3 * w:], (0, n_sm - n_used)).reshape(1, -1)
    lora_w = jnp.zeros((n_sm, 5 * w), F32)
    for d in range(2):
        lo = d * RWKV_DECAY_LORA
        lora_w = lora_w.at[lo:lo + RWKV_DECAY_LORA, d * w:(d + 1) * w].set(w_up[d])
        lo = 2 * RWKV_DECAY_LORA + d * RWKV_AAA_LORA
        lora_w = lora_w.at[lo:lo + RWKV_AAA_LORA, (2 + d) * w:(3 + d) * w].set(a_up[d])
    lo = 2 * RWKV_DECAY_LORA + 2 * RWKV_AAA_LORA
    lora_w = lora_w.at[lo:lo + RWKV_GATE_LORA, 4 * w:].set(g_up).astype(BF16)
    head = jnp.arange(w) // RWKV_HEAD_DIM
    seg = (head[:, None] == head[None, :]).astype(BF16)
    row = lambda t: t.reshape(1, -1)
    shared = [mu_main, mu_sm, row(k_k), row(k_a), row(r_k), w0, a0, lora_w, seg]
    prep = _token_call(_rwkv_prep_body, [rkv, sm], [rkv, sm], shared, [w] * 11, [F32] * 11, mod=None,
                       with_ctx=True, name="rwkv_prep", n_halo=2, seg=seg_len, joint_out=True)
    pr = [(a, w, 0) for a in prep]
    state = [pltpu.VMEM((RWKV_PAIRS * bsz, LANE, LANE), F32)]
    (y_bwd,) = _chunk_call_all(functools.partial(_rwkv_body, rev=True, last=False),
                               [pr[0], pr[6], pr[1], pr[2], pr[7], pr[8]], [], [w], [F32],
                               seg=seg_len, chunk=RWKV_CHUNK, rev=True, scratch=state, name="rwkv_bwd")
    ins = [pr[0], pr[3], pr[1], pr[2], pr[4], pr[5], (y_bwd, w, 0), pr[9], pr[10]]
    (out,) = _chunk_call_all(functools.partial(_rwkv_body, rev=False, last=True), ins,
                             [row(ln_w), row(ln_b), seg], [w], [BF16],
                             seg=seg_len, chunk=RWKV_CHUNK, rev=False, scratch=state, name="rwkv_fwd")
    return out


def _pad_cols(w, n):
    return jnp.pad(w, ((0, 0), (0, n - w.shape[1])))


def _mixer_cd(h_lat, h_ctx, mod, w_in, w_out, conv_w, a_log, dt_bias, norm_w, mu, w0, w_up, a0, a_up, g_up,
              k_k, k_a, r_k, ln_w, ln_b):
    n_qkv, n_gsm = 3 * GDN_WIDTH, 4 * GDN_HEADS
    n_rkv = 3 * RWKV_WIDTH
    lo_z, lo_g = n_qkv, n_qkv + GDN_WIDTH
    lo_r = lo_g + n_gsm
    lo_s = lo_r + n_rkv
    ws = [w_in[:, :n_qkv], w_in[:, lo_z:lo_g], _pad_cols(w_in[:, lo_g:lo_r], LANE),
          w_in[:, lo_r:lo_s], _pad_cols(w_in[:, lo_s:], 2 * LANE)]
    seg = (h_lat.shape[1], h_ctx.shape[1])
    p = _proj(h_lat, h_ctx, mod, 3, [w.astype(BF16) for w in ws], joint=True)
    d = _deltanet_branch(p[0], p[1], p[2], seg, conv_w, a_log, dt_bias, norm_w)
    r = _rwkv7_branch(p[3], p[4], seg, mu, w0, w_up, a0, a_up, g_up, k_k, k_a, r_k, ln_w, ln_b)
    return [d, r], [d, r], [w_out[:GDN_WIDTH].astype(BF16), w_out[GDN_WIDTH:].astype(BF16)]


def kernel(x, c, ctx, c_ctx, ada_w, ada_b, ffn_w_in, ffn_w_out, ln_g, ln_b, ab_w_in, ab_w_out, s5_lam_re, s5_lam_im, s5_log_dt, s5_b_re, s5_b_im, s5_c_re, s5_c_im, s5_d, s5_glu_w, s5_glu_b, ret_log_rate, cd_w_in, cd_w_out, gdn_conv_w, gdn_a_log, gdn_dt_bias, gdn_norm_w, rwkv_mu, rwkv_w0, rwkv_w_up, rwkv_a0, rwkv_a_up, rwkv_g_up, rwkv_k_k, rwkv_k_a, rwkv_r_k, rwkv_ln_w, rwkv_ln_b):
    mods = _adaln(c, c_ctx, ada_w, ada_b)
    h_lat, h_ctx = x, ctx
    for i in range(DEPTH):
        mod = mods[i]
        j = i // 2
        keep_ctx = i < DEPTH - 1
        h_lat, h_ctx = _ffn(h_lat, h_ctx, mod, 0, ffn_w_in[i, 0].astype(BF16), ffn_w_out[i, 0].astype(BF16),
                            ln_g[i, 0], ln_b[i, 0])
        if i % 2 == 0:
            s5_ops = _s5_operators(s5_lam_re[j], s5_lam_im[j], s5_log_dt[j], s5_b_re[j], s5_b_im[j],
                                   s5_c_re[j], s5_c_im[j])
            ms_lat, ms_ctx, w_out = _mixer_ab(h_lat, h_ctx, mod, ab_w_in[j], ab_w_out[j], s5_ops, s5_d[j],
                                              s5_glu_w[j], s5_glu_b[j], ret_log_rate[j])
        else:
            ms_lat, ms_ctx, w_out = _mixer_cd(
                h_lat, h_ctx, mod, cd_w_in[j], cd_w_out[j], gdn_conv_w[j], gdn_a_log[j], gdn_dt_bias[j],
                gdn_norm_w[j], rwkv_mu[j], rwkv_w0[j], rwkv_w_up[j], rwkv_a0[j], rwkv_a_up[j], rwkv_g_up[j],
                rwkv_k_k[j], rwkv_k_a[j], rwkv_r_k[j], rwkv_ln_w[j], rwkv_ln_b[j])
        h_lat, h_ctx = _ffn(h_lat, h_ctx, mod, 6, ffn_w_in[i, 1].astype(BF16), ffn_w_out[i, 1].astype(BF16),
                            ln_g[i, 2], ln_b[i, 2], with_ctx=keep_ctx,
                            mix=(ms_lat, ms_ctx, w_out, 5, ln_g[i, 1], ln_b[i, 1]))
    return h_lat
```

```python
import functools
import math

import jax
import jax.numpy as jnp
from jax import lax
from jax.experimental import pallas as pl
from jax.experimental.pallas import tpu as pltpu

F32 = jnp.float32
BF16 = jnp.bfloat16
HI = lax.Precision.HIGHEST

D_MODEL = 1024
DEPTH = 4
N_MOD = 9
D_FF = 2816
MACARON = 0.5
ALPHA = (2.0 * DEPTH) ** 0.25
LN_EPS = 1e-5
GRID_W = 64

LANE = 128
SUBLANE = 8
MXU_TILE = 256
VMEM_LIMIT_BYTES = 56 * 1024 * 1024

TOKEN_ROWS = 512
FF_CHUNK = MXU_TILE

def _cparams(*sem):
    return pltpu.CompilerParams(dimension_semantics=sem, vmem_limit_bytes=VMEM_LIMIT_BYTES)


def _full_spec(a):
    nd = a.ndim
    return pl.BlockSpec(a.shape, lambda *_, nd=nd: (0,) * nd, pipeline_mode=pl.Buffered(1))


def _dot(a, b):
    return jnp.dot(a, b, preferred_element_type=F32)


def _dot_nt(a, b):
    return lax.dot_general(a, b, (((1,), (1,)), ((), ())), preferred_element_type=F32)


def _dot_tn(a, b):
    return lax.dot_general(a, b, (((0,), (0,)), ((), ())), preferred_element_type=F32)


def _layer_norm(x, g, b):
    mu = jnp.mean(x, -1, keepdims=True)
    xc = x - mu
    var = jnp.mean(xc * xc, -1, keepdims=True)
    return xc * lax.rsqrt(var + LN_EPS) * g + b


def _adaln_kernel(s_ref, w_ref, b_ref, o_ref):
    s = s_ref[...]
    s = s * jax.nn.sigmoid(s)
    o_ref[0] = jnp.dot(s, w_ref[0], preferred_element_type=F32, precision=HI) + b_ref[0]


def _adaln(c, c_ctx, ada_w, ada_b):
    bsz = c.shape[0]
    rows = 2 * SUBLANE
    s = jnp.zeros((rows, D_MODEL), F32).at[:bsz].set(c).at[bsz].set(c_ctx)
    n_out = N_MOD * D_MODEL
    tn = D_MODEL
    out = pl.pallas_call(
        _adaln_kernel,
        grid=(DEPTH, n_out // tn),
        in_specs=[
            pl.BlockSpec((rows, D_MODEL), lambda i, j: (0, 0)),
            pl.BlockSpec((1, D_MODEL, tn), lambda i, j: (i, 0, j)),
            pl.BlockSpec((1, 1, tn), lambda i, j: (i, 0, j)),
        ],
        out_specs=pl.BlockSpec((1, rows, tn), lambda i, j: (i, 0, j)),
        out_shape=jax.ShapeDtypeStruct((DEPTH, rows, n_out), F32),
        compiler_params=_cparams("arbitrary", "arbitrary"),
        name="adaln",
    )(s, ada_w, ada_b.reshape(DEPTH, 1, n_out))
    return out.reshape(DEPTH, rows, N_MOD, D_MODEL)


def _token_call(body, lat_ins, ctx_ins, shared, out_widths, out_dtypes, *, mod, with_ctx, name,
                scratch_fn=None, n_halo=0, seg=None, joint_out=False):
    bsz = lat_ins[0].shape[0]
    if seg is None:
        seg = (lat_ins[0].shape[1], ctx_ins[0].shape[1] if with_ctx else 0)
    n_lat, n_ctx = seg
    tm = n_ctx if joint_out else TOKEN_ROWS
    nlb = n_lat // tm
    n_in, n_sh, n_out = len(lat_ins), len(shared), len(out_widths)
    sub_per_blk = tm // SUBLANE
    joint_in = [with_ctx and lat_ins[i] is ctx_ins[i] for i in range(n_in)]

    def lat_map(b, j):
        return (b, jnp.minimum(j, nlb - 1), 0)

    def ctx_map(b, j):
        return (b, 0, 0)

    def joint_ctx_map(b, j):
        return (b, n_lat // n_ctx, 0)

    def joint_out_map(b, j):
        return (b, j, 0)

    def prev_map(b, j):
        return (b, jnp.maximum(jnp.minimum(j, nlb - 1) * sub_per_blk - 1, 0), 0)

    def next_map(b, j):
        return (b, jnp.minimum((jnp.minimum(j, nlb - 1) + 1) * sub_per_blk, n_lat // SUBLANE - 1), 0)

    def per_batch(a, index_map):
        if a.shape[0] == bsz:
            return index_map
        return lambda b, j: (0,) + index_map(b, j)[1:]

    in_specs = [pl.BlockSpec((1, tm, a.shape[2]), per_batch(a, lat_map)) for a in lat_ins]
    args = list(lat_ins)
    for a in lat_ins[:n_halo]:
        in_specs += [pl.BlockSpec((1, SUBLANE, a.shape[2]), prev_map),
                     pl.BlockSpec((1, SUBLANE, a.shape[2]), next_map)]
        args += [a, a]
    if with_ctx:
        in_specs += [pl.BlockSpec((1, n_ctx, a.shape[2]), per_batch(a, joint_ctx_map if jt else ctx_map))
                     for a, jt in zip(ctx_ins, joint_in)]
        args += list(ctx_ins)
    n_mod = 0
    if mod is not None:
        in_specs.append(pl.BlockSpec((1, N_MOD, D_MODEL), lambda b, j: (b, 0, 0)))
        args.append(mod)
        n_mod = 1
        if with_ctx:
            in_specs.append(pl.BlockSpec((1, N_MOD, D_MODEL), lambda b, j: (bsz, 0, 0)))
            args.append(mod)
            n_mod = 2
    in_specs += [_full_spec(s) for s in shared]
    args += list(shared)

    if joint_out:
        out_specs = [pl.BlockSpec((1, tm, w), joint_out_map) for w in out_widths]
        out_shape = [jax.ShapeDtypeStruct((bsz, n_lat + n_ctx, w), dt) for w, dt in zip(out_widths, out_dtypes)]
    else:
        out_specs = [pl.BlockSpec((1, tm, w), lat_map) for w in out_widths]
        out_shape = [jax.ShapeDtypeStruct((bsz, n_lat, w), dt) for w, dt in zip(out_widths, out_dtypes)]
        if with_ctx:
            out_specs += [pl.BlockSpec((1, n_ctx, w), ctx_map) for w in out_widths]
            out_shape += [jax.ShapeDtypeStruct((bsz, n_ctx, w), dt) for w, dt in zip(out_widths, out_dtypes)]
    scratch = scratch_fn(tm) if scratch_fn is not None else []

    def kern(*refs):
        pos = 0
        lat_r = refs[pos:pos + n_in]; pos += n_in
        halo_r = refs[pos:pos + 2 * n_halo]; pos += 2 * n_halo
        ctx_r = ()
        if with_ctx:
            ctx_r = refs[pos:pos + n_in]; pos += n_in
        mod_r = refs[pos:pos + n_mod]; pos += n_mod
        sh_r = refs[pos:pos + n_sh]; pos += n_sh
        lat_o = refs[pos:pos + n_out]; pos += n_out
        ctx_o = lat_o
        if with_ctx and not joint_out:
            ctx_o = refs[pos:pos + n_out]; pos += n_out
        scr = refs[pos:]
        j = pl.program_id(1)

        def run(ins, modr, outs, halo):
            extra = (halo,) if n_halo else ()
            vals = body(ins, modr, sh_r, scr, *extra)
            for o, v in zip(outs, vals):
                o[0] = v.astype(o.dtype)

        lat_halo = (halo_r[0::2], halo_r[1::2], j > 0, j < nlb - 1) if n_halo else None
        if with_ctx:

            @pl.when(j < nlb)
            def _():
                run(lat_r, mod_r[0] if n_mod else None, lat_o, lat_halo)

            @pl.when(j == nlb)
            def _():
                run(ctx_r, mod_r[1] if n_mod else None, ctx_o, None)
        else:
            run(lat_r, mod_r[0] if n_mod else None, lat_o, lat_halo)

    outs = pl.pallas_call(
        kern,
        grid=(bsz, nlb + (1 if with_ctx else 0)),
        in_specs=in_specs,
        out_specs=out_specs,
        out_shape=out_shape,
        scratch_shapes=scratch,
        compiler_params=_cparams("arbitrary", "arbitrary"),
        name=name,
    )(*args)
    if joint_out:
        return list(outs)
    lat_out = list(outs[:n_out])
    ctx_out = list(outs[n_out:]) if with_ctx else [None] * n_out
    return lat_out, ctx_out


def _mod_rows(mod_ref, i):
    return mod_ref[0, i:i + 1, :], mod_ref[0, i + 1:i + 2, :], mod_ref[0, i + 2:i + 3, :]


def _ffn_body(ins, mod_ref, sh, scr, *, si, n_mix=0, gi=None):
    h = ins[0][0]
    if n_mix:
        gate = mod_ref[0, gi:gi + 1, :]
        y = _dot(ins[1][0], sh[0][...])
        for k in range(1, n_mix):
            y = y + _dot(ins[1 + k][0], sh[k][...])
        h = _layer_norm(ALPHA * h + gate * y, sh[n_mix][...], sh[n_mix + 1][...])
        sh = sh[n_mix + 2:]
    w_in, w_out, g, b = sh
    act = scr[0]
    shift, scale, gate = _mod_rows(mod_ref, si)
    rows = h.shape[0]
    xb = (h * (1.0 + scale) + shift).astype(BF16)
    for j in range(D_FF // FF_CHUNK):
        lo = j * FF_CHUNK
        a = _dot(xb, w_in[:, lo:lo + FF_CHUNK])
        bb = _dot(xb, w_in[:, D_FF + lo:D_FF + lo + FF_CHUNK])
        act[0:rows, lo:lo + FF_CHUNK] = (a * jax.nn.sigmoid(a) * bb).astype(BF16)
    y = _dot(act[0:rows, :], w_out[...])
    return [_layer_norm(ALPHA * h + gate * (MACARON * y), g[...], b[...])]


def _ffn(h_lat, h_ctx, mod, si, w_in, w_out, g, b, with_ctx=True, mix=None):
    row = lambda t: t.reshape(1, -1)
    lat_ins, ctx_ins, shared, kw = [h_lat], [h_ctx], [], {}
    if mix is not None:
        ms_lat, ms_ctx, ws, gi, g_mix, b_mix = mix
        lat_ins, ctx_ins = lat_ins + list(ms_lat), ctx_ins + list(ms_ctx)
        shared = list(ws) + [row(g_mix), row(b_mix)]
        kw = dict(n_mix=len(ws), gi=gi)
    seg = (h_lat.shape[1], h_ctx.shape[1] if with_ctx else 0)
    lat, ctx = _token_call(
        functools.partial(_ffn_body, si=si, **kw), lat_ins, ctx_ins, shared + [w_in, w_out, row(g), row(b)],
        [D_MODEL], [F32], mod=mod, with_ctx=with_ctx, name="ffn_half_step", seg=seg,
        scratch_fn=lambda tm: [pltpu.VMEM((tm, D_FF), BF16)])
    return lat[0], ctx[0]


def _chunk_call_all(body, ins, shared, out_widths, out_dtypes, *, seg, chunk, rev, scratch, name,
                    per_batch=False):
    n_lat, n_ctx = seg
    bsz = ins[0][0].shape[0]
    ncl, ncc = n_lat // chunk, n_ctx // chunk
    n_all = ncl + ncc
    rows = 1 if per_batch else bsz

    def blk(n):
        if rev:
            return n_all - 1 - n
        return jnp.where(n < ncc, ncl + n, n - ncc)

    def index(cb):
        if per_batch:
            return lambda b, n: (b, blk(n), cb)
        return lambda n: (0, blk(n), cb)

    in_specs = [pl.BlockSpec((rows, chunk, w), index(cb)) for _, w, cb in ins]
    in_specs += [_full_spec(s) for s in shared]
    out_specs = [pl.BlockSpec((rows, chunk, w), index(0)) for w in out_widths]
    out_shape = [jax.ShapeDtypeStruct((bsz, n_lat + n_ctx, w), dt) for w, dt in zip(out_widths, out_dtypes)]
    n_i, n_s, n_o = len(ins), len(shared), len(out_widths)

    def kern(*refs):
        in_r, sh = refs[:n_i], refs[n_i:n_i + n_s]
        out_r = refs[n_i + n_s:n_i + n_s + n_o]
        scr = refs[n_i + n_s + n_o:]
        outs = body(pl.program_id(1 if per_batch else 0), [r[...] for r in in_r], sh, scr)
        for o, v in zip(out_r, outs):
            o[...] = v.astype(o.dtype)

    return pl.pallas_call(
        kern,
        grid=(bsz, n_all) if per_batch else (n_all,),
        in_specs=in_specs,
        out_specs=out_specs,
        out_shape=out_shape,
        scratch_shapes=scratch,
        compiler_params=_cparams(*(["arbitrary"] * (2 if per_batch else 1))),
        name=name,
    )(*[a for a, _, _ in ins], *shared)


S5_WIDTH = 256
S5_GROUP = 16
S5_GROUPS = 16
S5_STATE = 64
S5_T = 4
S5_ROW = S5_T * S5_WIDTH
S5_NSTATE = S5_GROUPS * S5_STATE


def _s5_operators(lam_re, lam_im, log_dt, b_re, b_im, c_re, c_im):
    t_len, g_n, p_n, c_n = S5_T, S5_GROUPS, S5_STATE, S5_GROUP
    dt = jnp.exp(log_dt)[..., None]
    a_r, a_i = lam_re * dt, lam_im * dt
    mag = jnp.exp(a_r)
    lb_re, lb_im = mag * jnp.cos(a_i), mag * jnp.sin(a_i)
    den = lam_re * lam_re + lam_im * lam_im
    nr = lb_re - 1.0
    coef_re = (nr * lam_re + lb_im * lam_im) / den
    coef_im = (lb_im * lam_re - nr * lam_im) / den
    bb_re = coef_re[..., None] * b_re - coef_im[..., None] * b_im
    bb_im = coef_re[..., None] * b_im + coef_im[..., None] * b_re

    def lam_pow(e):
        m = jnp.exp(a_r * e)
        return m * jnp.cos(a_i * e), m * jnp.sin(a_i * e)

    eye = jnp.eye(g_n, dtype=F32)

    def kern_tau(d, tau):
        pr, pi = lam_pow(float(tau))
        cp_re = c_re * pr[d][:, None, :] - c_im * pi[d][:, None, :]
        cp_im = c_re * pi[d][:, None, :] + c_im * pr[d][:, None, :]
        k = (jnp.einsum("gcp,gpe->gec", cp_re, bb_re[d], precision=HI)
             - jnp.einsum("gcp,gpe->gec", cp_im, bb_im[d], precision=HI))
        return k

    k0 = [kern_tau(0, tau) for tau in range(t_len)]
    k1 = [kern_tau(1, tau) for tau in range(t_len)]
    zero = jnp.zeros_like(k0[0])
    rows = []
    for s in range(t_len):
        cols = []
        for t in range(t_len):
            blk = zero
            if s <= t:
                blk = blk + k0[t - s]
            if s >= t:
                blk = blk + k1[s - t]
            cols.append(blk)
        rows.append(jnp.stack(cols, 0))
    kst = jnp.stack(rows, 0)
    m_op = jnp.einsum("stgec,gh->sgethc", kst, eye).reshape(S5_ROW, S5_ROW)

    w_ops, v_ops, lam_t = [], [], []
    for d in range(2):
        w_re, w_im, v_re, v_im = [], [], [], []
        for s in range(t_len):
            pr, pi = lam_pow(float(t_len - 1 - s) if d == 0 else float(s))
            w_re.append(pr[d][..., None] * bb_re[d] - pi[d][..., None] * bb_im[d])
            w_im.append(pr[d][..., None] * bb_im[d] + pi[d][..., None] * bb_re[d])
        for t in range(t_len):
            pr, pi = lam_pow(float(t + 1) if d == 0 else float(t_len - t))
            v_re.append(c_re * pr[d][:, None, :] - c_im * pi[d][:, None, :])
            v_im.append(-(c_re * pi[d][:, None, :] + c_im * pr[d][:, None, :]))
        for w in (w_re, w_im):
            w_ops.append(jnp.einsum("sgpe,gh->sgehp", jnp.stack(w, 0), eye).reshape(S5_ROW, S5_NSTATE))
        for v in (v_re, v_im):
            v_ops.append(jnp.einsum("tgcp,gh->gpthc", jnp.stack(v, 0), eye).reshape(S5_NSTATE, S5_ROW))
        pr, pi = lam_pow(float(t_len))
        lam_t.append((pr[d].reshape(1, S5_NSTATE), pi[d].reshape(1, S5_NSTATE)))
    w_cat = jnp.concatenate([m_op] + w_ops, axis=1).astype(BF16)
    return w_cat, [v.astype(BF16) for v in v_ops], lam_t


def _s5_drive_body(ins, mod_ref, sh, scr):
    xb = ins[0][0].astype(BF16)
    w = sh[0]
    n = w.shape[1] // S5_ROW
    return [_dot(xb, w[:, i * S5_ROW:(i + 1) * S5_ROW]) for i in range(n)]


def _s5_state_scan(b_re, b_im, lam, rev):
    (bre_l, bre_c), (bim_l, bim_c) = b_re, b_im
    bsz, nl, w = bre_l.shape
    nc = bre_c.shape[1]

    def flat(a):
        return a.reshape(bsz * a.shape[1], w)

    def kern(brl, bil, brc, bic, lr_ref, li_ref, hrl, hil, hrc, hic):
        lr = jnp.broadcast_to(lr_ref[...], (bsz, LANE))
        li = jnp.broadcast_to(li_ref[...], (bsz, LANE))

        def make(br, bi, hr_o, hi_o, nrows):
            def step(i, carry):
                hr, hi = carry
                n = (nrows - 1 - i) if rev else i
                idx = pl.ds(n, bsz, stride=nrows)
                hr_o[idx, :] = hr
                hi_o[idx, :] = hi
                xr, xi = br[idx, :], bi[idx, :]
                return lr * hr - li * hi + xr, lr * hi + li * hr + xi
            return step

        z = jnp.zeros((bsz, LANE), F32)
        carry = lax.fori_loop(0, nc, make(brc, bic, hrc, hic, nc), (z, z), unroll=8)
        lax.fori_loop(0, nl, make(brl, bil, hrl, hil, nl), carry, unroll=8)

    def spec(rows):
        return pl.BlockSpec((rows, LANE), lambda j: (0, j))

    lam_spec = pl.BlockSpec((1, LANE), lambda j: (0, j))
    outs = pl.pallas_call(
        kern,
        grid=(w // LANE,),
        in_specs=[spec(bsz * nl), spec(bsz * nl), spec(bsz * nc), spec(bsz * nc), lam_spec, lam_spec],
        out_specs=[spec(bsz * nl), spec(bsz * nl), spec(bsz * nc), spec(bsz * nc)],
        out_shape=[jax.ShapeDtypeStruct((bsz * nl, w), F32), jax.ShapeDtypeStruct((bsz * nl, w), F32),
                   jax.ShapeDtypeStruct((bsz * nc, w), F32), jax.ShapeDtypeStruct((bsz * nc, w), F32)],
        compiler_params=_cparams("arbitrary"),
        name="s5_state_scan",
    )(flat(bre_l), flat(bim_l), flat(bre_c), flat(bim_c), lam[0], lam[1])
    hrl, hil, hrc, hic = outs
    return (hrl.reshape(bsz, nl, w), hrc.reshape(bsz, nc, w)), (hil.reshape(bsz, nl, w), hic.reshape(bsz, nc, w))


def _s5_finish_body(ins, mod_ref, sh, scr):
    yi, h0r, h0i, h1r, h1i, u = [r[0] for r in ins]
    v0r, v0i, v1r, v1i, d_skip, glu_w, glu_b = sh
    y = yi + d_skip[...] * u
    for h, v in ((h0r, v0r), (h0i, v0i), (h1r, v1r), (h1i, v1i)):
        y = y + _dot(h.astype(BF16), v[...])
    z = jax.nn.gelu(y)
    return [z * jax.nn.sigmoid(_dot(z.astype(BF16), glu_w[...]) + glu_b[...])]


def _s5_branch(u, seg, ops, d_skip, glu_w, glu_b):
    w_cat, v_ops, lam_t = ops
    bsz = u.shape[0]
    n_lat, n_ctx = seg
    seg4 = (n_lat // S5_T, n_ctx // S5_T)
    u4l = u4c = u.reshape(bsz, seg4[0] + seg4[1], S5_ROW)
    lat, ctx = _token_call(_s5_drive_body, [u4l], [u4c], [w_cat], [S5_ROW] * 5, [F32] * 5,
                           mod=None, with_ctx=True, name="s5_drive", seg=seg4)
    pairs = list(zip(lat, ctx))
    h0 = _s5_state_scan(pairs[1], pairs[2], lam_t[0], rev=False)
    h1 = _s5_state_scan(pairs[3], pairs[4], lam_t[1], rev=True)
    eye_t = jnp.eye(S5_T, dtype=F32)
    glu_k = jnp.kron(eye_t, glu_w).astype(BF16)
    fin_l = [pairs[0][0], h0[0][0], h0[1][0], h1[0][0], h1[1][0], u4l]
    fin_c = [pairs[0][1], h0[0][1], h0[1][1], h1[0][1], h1[1][1], u4c]
    shared = list(v_ops) + [jnp.tile(d_skip, S5_T).reshape(1, S5_ROW), glu_k, jnp.tile(glu_b, S5_T).reshape(1, S5_ROW)]
    lat, ctx = _token_call(_s5_finish_body, fin_l, fin_c, shared, [S5_ROW], [BF16],
                           mod=None, with_ctx=True, name="s5_finish", seg=seg4)
    return lat[0].reshape(bsz, n_lat, S5_WIDTH), ctx[0].reshape(bsz, n_ctx, S5_WIDTH)


RET_HEADS = 6
RET_HEAD_DIM = 128
RET_WIDTH = RET_HEADS * RET_HEAD_DIM
RET_CHUNK = 256
ROPE_BASE = 10000.0


def _rope_tables(n_lat):
    nf = RET_HEAD_DIM // 4
    rows = n_lat // GRID_W
    freqs = ROPE_BASE ** (-jnp.arange(nf, dtype=F32) / nf)
    pr = jnp.broadcast_to(jnp.arange(rows, dtype=F32)[:, None], (rows, GRID_W)).reshape(-1)
    pc = jnp.broadcast_to(jnp.arange(GRID_W, dtype=F32)[None, :], (rows, GRID_W)).reshape(-1)
    ang = jnp.concatenate([pr[:, None] * freqs, pc[:, None] * freqs], -1)
    cos, sin = jnp.cos(ang), jnp.sin(ang)
    return jnp.concatenate([cos, cos], -1), jnp.concatenate([-sin, sin], -1)


def _ret_tables(log_rate):
    c = RET_CHUNK
    lg = -jnp.exp(log_rate.astype(F32))
    idx = jnp.arange(c, dtype=F32)
    diff = idx[:, None] - idx[None, :]
    past, fut = diff >= 0, diff <= 0
    d0 = jnp.where(past, jnp.exp(jnp.where(past, diff, 0.0)[None] * lg[0][:, None, None]), 0.0)
    d1 = jnp.where(fut, jnp.exp(jnp.where(fut, -diff, 0.0)[None] * lg[1][:, None, None]), 0.0)

    def rep(t):
        return jnp.repeat(t, RET_HEAD_DIM, axis=-1)

    fwd = (rep(jnp.exp((idx + 1.0)[:, None] * lg[0])), rep(jnp.exp((c - 1.0 - idx)[:, None] * lg[0])),
           rep(jnp.exp(c * lg[0])[None]))
    bwd = (rep(jnp.exp((c - idx)[:, None] * lg[1])), rep(jnp.exp(idx[:, None] * lg[1])),
           rep(jnp.exp(c * lg[1])[None]))
    return d0 + d1, fwd, bwd


def _rope(x, cos, sin):
    return x * cos + pltpu.roll(x, RET_HEAD_DIM // 2, 1) * sin


def _proj_rope_body(ins, mod_ref, sh, scr, *, si):
    h, cos, sin = ins[0][0], ins[1][0], ins[2][0]
    shift, scale, _ = _mod_rows(mod_ref, si)
    xb = (h * (1.0 + scale) + shift).astype(BF16)

    def rotated(w, mult):
        y = _dot(xb, w[...])
        cols = [_rope(y[:, i * RET_HEAD_DIM:(i + 1) * RET_HEAD_DIM], cos, sin) * mult for i in range(RET_HEADS)]
        return jnp.concatenate(cols, -1)

    return [rotated(sh[0], 1.0), rotated(sh[1], RET_HEAD_DIM ** -0.5)] + [_dot(xb, w[...]) for w in sh[2:]]


def _ret_bwd_body(n, vals, sh, scr):
    q, k, v = (t[0] for t in vals)
    xi, zeta, gc = sh
    state = scr[0]

    @pl.when(n == 0)
    def _():
        state[...] = jnp.zeros_like(state)

    outs = []
    for h in range(RET_HEADS):
        sl = slice(h * RET_HEAD_DIM, (h + 1) * RET_HEAD_DIM)
        s_old = state[h]
        outs.append(_dot((q[:, sl] * xi[:, sl]).astype(BF16), s_old.astype(BF16)))
        state[h] = gc[:, sl] * s_old + _dot_tn((k[:, sl] * zeta[:, sl]).astype(BF16), v[:, sl].astype(BF16))
    return [jnp.concatenate(outs, -1)[None]]


def _ret_fwd_body(n, vals, sh, scr):
    q, k, v, g, o_bwd = (t[0] for t in vals)
    dcomb, xi, zeta, gc = sh
    state = scr[0]

    @pl.when(n == 0)
    def _():
        state[...] = jnp.zeros_like(state)

    outs = []
    for h in range(RET_HEADS):
        sl = slice(h * RET_HEAD_DIM, (h + 1) * RET_HEAD_DIM)
        qh, kh = q[:, sl], k[:, sl]
        vb = v[:, sl].astype(BF16)
        s_old = state[h]
        scores = _dot_nt(qh.astype(BF16), kh.astype(BF16)) * dcomb[h]
        o = (_dot(scores.astype(BF16), vb) + _dot((qh * xi[:, sl]).astype(BF16), s_old.astype(BF16))
             + o_bwd[:, sl])
        state[h] = gc[:, sl] * s_old + _dot_tn((kh * zeta[:, sl]).astype(BF16), vb)
        mu = jnp.mean(o, -1, keepdims=True)
        oc = o - mu
        var = jnp.mean(oc * oc, -1, keepdims=True)
        gh = g[:, sl]
        outs.append(gh * jax.nn.sigmoid(gh) * (oc * lax.rsqrt(var + LN_EPS)))
    return [jnp.concatenate(outs, -1)[None]]


def _retention_branch(q, k, vg, seg, log_rate):
    dcomb, fwd, bwd = _ret_tables(log_rate)
    w = RET_WIDTH
    state = [pltpu.VMEM((RET_HEADS, RET_HEAD_DIM, RET_HEAD_DIM), F32)]
    qkv = [(q, w, 0), (k, w, 0), (vg, w, 0)]
    (o_bwd,) = _chunk_call_all(_ret_bwd_body, qkv, list(bwd), [w], [F32], seg=seg, chunk=RET_CHUNK,
                               rev=True, scratch=state, name="retention_bwd", per_batch=True)
    ins = qkv + [(vg, w, 1), (o_bwd, w, 0)]
    (r,) = _chunk_call_all(_ret_fwd_body, ins, [dcomb] + list(fwd), [w], [BF16], seg=seg, chunk=RET_CHUNK,
                           rev=False, scratch=state, name="retention_fwd", per_batch=True)
    return r


def _mixer_ab(h_lat, h_ctx, mod, w_in, w_out, s5_ops, s5_d, s5_glu_w, s5_glu_b, ret_log_rate):
    bsz, n_lat, _ = h_lat.shape
    n_ctx = h_ctx.shape[1]
    w = RET_WIDTH
    lo_q, lo_k, lo_v = S5_WIDTH, S5_WIDTH + w, S5_WIDTH + 2 * w
    ws = [w_in[:, lo_q:lo_k], w_in[:, lo_k:lo_v], w_in[:, lo_v:], w_in[:, :S5_WIDTH]]
    cos, sin = _rope_tables(n_lat)
    lat_ins = [h_lat, cos[None], sin[None]]
    ctx_ins = [h_ctx, jnp.ones((1, n_ctx, RET_HEAD_DIM), F32), jnp.zeros((1, n_ctx, RET_HEAD_DIM), F32)]
    q, k, vg, u = _token_call(functools.partial(_proj_rope_body, si=3), lat_ins, ctx_ins,
                              [t.astype(BF16) for t in ws], [w, w, 2 * w, S5_WIDTH], [F32] * 4,
                              mod=mod, with_ctx=True, name="mixer_in_proj", joint_out=True)
    a_lat, a_ctx = _s5_branch(u, (n_lat, n_ctx), s5_ops, s5_d, s5_glu_w, s5_glu_b)
    r = _retention_branch(q, k, vg, (n_lat, n_ctx), ret_log_rate)
    return [a_lat, r], [a_ctx, r], [w_out[:S5_WIDTH].astype(BF16), w_out[S5_WIDTH:].astype(BF16)]


def _mm(a, b, exact=False):
    if exact:
        return jnp.dot(a, b, preferred_element_type=F32, precision=HI)
    return _dot(a.astype(BF16), b.astype(BF16))


def _softplus(x):
    return jnp.maximum(x, 0.0) + jnp.log1p(jnp.exp(-jnp.abs(x)))


def _shift_rows(x, s, prev8, next8):
    rows = x.shape[0]
    rolled = pltpu.roll(x, (-s) % rows, 0)
    r8 = lax.broadcasted_iota(jnp.int32, (SUBLANE, x.shape[1]), 0)
    if s < 0:
        edge = jnp.where(r8 < -s, pltpu.roll(prev8, -s, 0), rolled[:SUBLANE])
        return jnp.concatenate([edge, rolled[SUBLANE:]], 0)
    edge = jnp.where(r8 >= SUBLANE - s, pltpu.roll(next8, SUBLANE - s, 0), rolled[rows - SUBLANE:])
    return jnp.concatenate([rolled[:rows - SUBLANE], edge], 0)


def _order_masks(c, rev, reps=1):
    ri = lax.broadcasted_iota(jnp.int32, (c, reps * c), 0)
    ci = jnp.bitwise_and(lax.broadcasted_iota(jnp.int32, (c, reps * c), 1), c - 1)
    if rev:
        return ri <= ci, ri < ci
    return ri >= ci, ri > ci


TRI_BASE = 8


def _bmm(a, b):
    return lax.dot_general(a.astype(BF16), b.astype(BF16), (((2,), (1,)), ((0,), (0,))),
                           preferred_element_type=F32)


def _bmm_nt(a, b):
    return lax.dot_general(a.astype(BF16), b.astype(BF16), (((2,), (2,)), ((0,), (0,))),
                           preferred_element_type=F32)


def _bmm_tn(a, b):
    return lax.dot_general(a.astype(BF16), b.astype(BF16), (((1,), (1,)), ((0,), (0,))),
                           preferred_element_type=F32)


def _tri_inverse(n_mat, mul):
    c, lanes = n_mat.shape[1], n_mat.shape[2]
    ri = lax.broadcasted_iota(jnp.int32, (c, lanes), 0)
    ci = jnp.bitwise_and(lax.broadcasted_iota(jnp.int32, (c, lanes), 1), c - 1)

    def same_block(bits):
        return jnp.right_shift(ri, bits) == jnp.right_shift(ci, bits)

    bits = TRI_BASE.bit_length() - 1
    same = same_block(bits)
    p = jnp.where(same, n_mat, 0.0)
    x = jnp.where(ri == ci, 1.0, 0.0) + p
    p = mul(p, p)
    k = 2
    while 2 * k < TRI_BASE:
        t = mul(jnp.concatenate([x, p], axis=1), p)
        x, p = x + t[:, :c], t[:, c:]
        k *= 2
    x = x + mul(x, p)
    while (1 << bits) < c:
        bits += 1
        same2 = same_block(bits)
        off = jnp.where(jnp.logical_and(same2, jnp.logical_not(same)), n_mat, 0.0)
        x = x + mul(mul(x, off), x)
        same = same2
    return x


def _heads_to_batch(x, n):
    w = x.shape[2] // n
    return jnp.concatenate([x[:, :, i * w:(i + 1) * w] for i in range(n)], axis=0)


def _batch_to_heads(x, n):
    b = x.shape[0] // n
    return jnp.concatenate([x[i * b:(i + 1) * b] for i in range(n)], axis=-1)


def _scan_cumsum(tri, x):
    bsz, _, w = x.shape
    flat = jnp.concatenate([x[b] for b in range(bsz)], axis=-1)
    cum = _mm(tri, flat, exact=True)
    return jnp.stack([cum[:, b * w:(b + 1) * w] for b in range(bsz)], axis=0)


def _seg_sum(x, seg):
    hi = x.astype(BF16)
    lo = (x - hi.astype(F32)).astype(BF16)
    return _dot(hi, seg) + _dot(lo, seg)


GDN_HEADS = 4
GDN_HEAD_DIM = 128
GDN_WIDTH = GDN_HEADS * GDN_HEAD_DIM
GDN_CONV = 5
GDN_CHUNK = 64


def _gdn_prep(qkv, sm, prev8, next8, conv_w, a_neg, dt_bias):
    half = GDN_CONV // 2
    acc = conv_w[half:half + 1, :] * qkv
    for i in range(GDN_CONV):
        if i != half:
            acc = acc + conv_w[i:i + 1, :] * _shift_rows(qkv, i - half, prev8, next8)
    y = acc * jax.nn.sigmoid(acc)
    outs = []
    for part, scale in ((0, GDN_HEAD_DIM ** -0.5), (1, 1.0)):
        cols = []
        for h in range(GDN_HEADS):
            lo = part * GDN_WIDTH + h * GDN_HEAD_DIM
            t = y[:, lo:lo + GDN_HEAD_DIM]
            cols.append(t * (lax.rsqrt(jnp.sum(t * t, -1, keepdims=True) + 1e-6) * scale))
        outs.append(jnp.concatenate(cols, -1))
    outs.append(y[:, 2 * GDN_WIDTH:])
    lane = lax.broadcasted_iota(jnp.int32, sm.shape, 1)
    g = a_neg[...] * _softplus(sm + dt_bias[...])
    outs.append(jnp.where(lane < 2 * GDN_HEADS, g, jax.nn.sigmoid(sm)))
    return outs


def _gdn_body(n, vals, sh, scr, *, rev, last):
    q, k, v, sm = vals[:4]
    state = scr[0]
    c = GDN_CHUNK
    bsz = q.shape[0]

    @pl.when(n == 0)
    def _():
        state[...] = jnp.zeros_like(state)

    later, strict = _order_masks(c, rev)
    gcum = _scan_cumsum(later.astype(F32), sm)
    gcum_t = [gcum[b].T for b in range(bsz)]
    end = 0 if rev else c - 1
    d = 1 if rev else 0
    g_col, g_row, beta = [], [], []
    for h in range(GDN_HEADS):
        col = d * GDN_HEADS + h
        bcol = 2 * GDN_HEADS + col
        for b in range(bsz):
            g_col.append(gcum[b][:, col:col + 1])
            g_row.append(gcum_t[b][col:col + 1, :])
            beta.append(sm[b][:, bcol:bcol + 1])
    gc, gr, beta = jnp.stack(g_col), jnp.stack(g_row), jnp.stack(beta)
    dec = jnp.where(later, jnp.exp(jnp.where(later, gc - gr, 0.0)), 0.0)
    qg, kg, vg = (_heads_to_batch(t, GDN_HEADS) for t in (q, k, v))
    kb = kg * beta
    scores = _bmm_nt(jnp.concatenate([kb, qg], axis=1), kg)
    a_mat = jnp.where(strict, scores[:, :c] * dec, 0.0)
    attn = scores[:, c:] * dec
    t_mat = _tri_inverse(-a_mat, _bmm)
    eg = jnp.exp(gc)
    wu = _bmm(t_mat, jnp.concatenate([kb * eg, vg * beta], axis=-1))
    w, u = wu[:, :, :GDN_HEAD_DIM], wu[:, :, GDN_HEAD_DIM:]
    g_end = gc[:, end:end + 1, :]
    s_old = state[...]
    ws = _bmm(jnp.concatenate([w, qg * eg], axis=1), s_old)
    v_new = u - ws[:, :c]
    o = ws[:, c:] + _bmm(attn, v_new)
    state[...] = s_old * jnp.exp(g_end) + _bmm_tn(kg * jnp.exp(g_end - gc), v_new)
    o = _batch_to_heads(o, GDN_HEADS)
    if not last:
        return [o]
    z, o_prev = vals[4], vals[5]
    norm_w = sh[0]
    o = o + o_prev
    cols = []
    for h in range(GDN_HEADS):
        t = o[:, :, h * GDN_HEAD_DIM:(h + 1) * GDN_HEAD_DIM]
        cols.append(t * lax.rsqrt(jnp.mean(t * t, -1, keepdims=True) + 1e-6) * norm_w[...])
    return [jnp.concatenate(cols, -1) * (z * jax.nn.sigmoid(z))]


def _gdn_params(a_log, dt_bias, n_sm):
    pad = n_sm - 2 * GDN_HEADS
    a_neg = jnp.pad(-jnp.exp(a_log.astype(F32)).reshape(1, -1), ((0, 0), (0, pad)))
    return a_neg, jnp.pad(dt_bias.astype(F32).reshape(1, -1), ((0, 0), (0, pad)))


def _deltanet_branch(prep, z, seg, norm_w):
    bsz = z.shape[0]
    w = GDN_WIDTH
    ins = [(a, a.shape[2], 0) for a in prep]
    state = [pltpu.VMEM((GDN_HEADS * bsz, GDN_HEAD_DIM, GDN_HEAD_DIM), F32)]
    (o_bwd,) = _chunk_call_all(functools.partial(_gdn_body, rev=True, last=False), ins, [], [w], [F32],
                               seg=seg, chunk=GDN_CHUNK, rev=True, scratch=state, name="gdn_bwd")
    ins = ins + [(z, w, 0), (o_bwd, w, 0)]
    (out,) = _chunk_call_all(functools.partial(_gdn_body, rev=False, last=True), ins, [norm_w.reshape(1, 1, -1)],
                             [w], [BF16], seg=seg, chunk=GDN_CHUNK, rev=False, scratch=state, name="gdn_fwd")
    return out


RWKV_HEADS = 8
RWKV_HEAD_DIM = 64
RWKV_WIDTH = RWKV_HEADS * RWKV_HEAD_DIM
RWKV_DECAY_LORA = 32
RWKV_AAA_LORA = 32
RWKV_GATE_LORA = 96
RWKV_GN_EPS = 64e-5
RWKV_CHUNK = 64
RWKV_LOG_DECAY_SCALE = -math.exp(-0.5)
RWKV_PAIRS = RWKV_WIDTH // LANE


def _rwkv_prep(rkv, sm, halo_rkv, halo_sm, sh):
    mu_main, mu_sm, k_k, k_a, r_k, w0, a0, lora_w, seg = sh
    w = RWKV_WIDTH

    def lerp(x, mu, halo):
        xs = 0.5 * (_shift_rows(x, -1, *halo) + _shift_rows(x, 1, *halo))
        return x + mu[...] * (xs - x)

    rkv = lerp(rkv, mu_main, halo_rkv)
    sm = lerp(sm, mu_sm, halo_sm)
    r, k, v = rkv[:, :w], rkv[:, w:2 * w], rkv[:, 2 * w:]
    lane = lax.broadcasted_iota(jnp.int32, sm.shape, 1)
    n_dec, n_aaa = 2 * RWKV_DECAY_LORA, 2 * RWKV_AAA_LORA
    t = jnp.where(lane < n_dec, jnp.tanh(sm), jnp.where(lane < n_dec + n_aaa, sm, jax.nn.sigmoid(sm)))
    lo = _dot(t.astype(BF16), lora_w[...])
    kk = k * k_k[...]
    kk = kk * lax.rsqrt(_seg_sum(kk * kk, seg[...]) + 1e-6)
    outs = [r, v, kk]
    k_sum = None
    for d in range(2):
        log_w = RWKV_LOG_DECAY_SCALE * jax.nn.sigmoid(w0[d:d + 1, :] + lo[:, d * w:(d + 1) * w])
        a = jax.nn.sigmoid(a0[d:d + 1, :] + lo[:, (2 + d) * w:(3 + d) * w])
        k_d = k * (1.0 + (a - 1.0) * k_a[...])
        outs += [k_d, log_w, kk * a]
        k_sum = k_d if k_sum is None else k_sum + k_d
    outs.append(lo[:, 4 * w:])
    outs.append(_seg_sum(r * k_sum * r_k[...], seg[...]) * v)
    return outs


def _head_rows(x):
    first = lax.broadcasted_iota(jnp.int32, x.shape[1:], 1) < RWKV_HEAD_DIM
    return jnp.concatenate([jnp.where(first, x, 0.0), jnp.where(first, 0.0, x)], axis=1)


def _pair_mul(x, y):
    return _bmm(x, _head_rows(y))


def _rwkv_body(n, vals, sh, scr, *, rev, last):
    r, k, v, kk, lw, b = vals[:6]
    state = scr[0]
    c = RWKV_CHUNK
    bsz = r.shape[0]

    @pl.when(n == 0)
    def _():
        state[...] = jnp.zeros_like(state)

    later, _ = _order_masks(c, rev)
    later2, strict2 = _order_masks(c, rev, reps=2)
    cum = _scan_cumsum(later.astype(F32), lw)
    end = 0 if rev else c - 1
    cum_end = cum[:, end:end + 1, :]
    e_inv = jnp.exp(-cum)
    e_tail = jnp.exp(cum_end - cum)
    to_pairs = functools.partial(_heads_to_batch, n=RWKV_PAIRS)
    r_t = to_pairs(r * jnp.exp(cum))
    a_t = to_pairs(-kk * jnp.exp(cum - lw))
    b_t, k_t = to_pairs(b * e_inv), to_pairs(k * e_inv)
    b_h, k_h = to_pairs(b * e_tail), to_pairs(k * e_tail)
    p_end = to_pairs(jnp.exp(cum_end))
    v_p = to_pairs(v)
    prods = _bmm_nt(jnp.concatenate([a_t, r_t], axis=1),
                    jnp.concatenate([_head_rows(b_t), _head_rows(k_t)], axis=1))
    a_ab = jnp.where(strict2, prods[:, :c, :LANE], 0.0)
    a_ak = jnp.where(strict2, prods[:, :c, LANE:], 0.0)
    a_rb = jnp.where(later2, prods[:, c:, :LANE], 0.0)
    a_rk = jnp.where(later2, prods[:, c:, LANE:], 0.0)
    inv = _tri_inverse(a_ab, _pair_mul)
    akv = _pair_mul(a_ak, v_p)
    wu = _bmm(inv, jnp.concatenate([_head_rows(a_t), _head_rows(akv)], axis=-1))
    w_t, u_t = wu[:, :, :LANE], wu[:, :, LANE:]
    s_old = state[...]
    ws = _bmm_nt(jnp.concatenate([w_t, r_t], axis=1), s_old)
    u = ws[:, :c] + u_t
    y = ws[:, c:] + _bmm(jnp.concatenate([a_rb, a_rk], axis=-1),
                         jnp.concatenate([_head_rows(u), _head_rows(v_p)], axis=1))
    s_new = s_old * p_end + _bmm_tn(jnp.concatenate([u, v_p], axis=1), jnp.concatenate([b_h, k_h], axis=1))
    bi = lax.broadcasted_iota(jnp.int32, (LANE, LANE), 0) < RWKV_HEAD_DIM
    bj = lax.broadcasted_iota(jnp.int32, (LANE, LANE), 1) < RWKV_HEAD_DIM
    state[...] = jnp.where(bi == bj, s_new, 0.0)
    y = _batch_to_heads(y, RWKV_PAIRS)
    if not last:
        return [y]
    y_prev, g, bonus = vals[6:9]
    ln_w, ln_b, seg = sh
    w = RWKV_WIDTH
    y = (y + y_prev).reshape(bsz * c, w)
    inv_n = 1.0 / RWKV_HEAD_DIM
    mu = _seg_sum(y, seg[...]) * inv_n
    yc = y - mu
    var = _seg_sum(yc * yc, seg[...]) * inv_n
    out = (yc * lax.rsqrt(var + RWKV_GN_EPS)) * ln_w[...] + ln_b[...]
    return [(out.reshape(bsz, c, w) + bonus) * g]


def _rwkv_params(n_sm, mu, w0, w_up, a0, a_up, g_up, k_k, k_a, r_k):
    w = RWKV_WIDTH
    n_used = 2 * RWKV_DECAY_LORA + 2 * RWKV_AAA_LORA + RWKV_GATE_LORA
    mu_main = mu[:3 * w].reshape(1, -1)
    mu_sm = jnp.pad(mu[3 * w:], (0, n_sm - n_used)).reshape(1, -1)
    lora_w = jnp.zeros((n_sm, 5 * w), F32)
    for d in range(2):
        lo = d * RWKV_DECAY_LORA
        lora_w = lora_w.at[lo:lo + RWKV_DECAY_LORA, d * w:(d + 1) * w].set(w_up[d])
        lo = 2 * RWKV_DECAY_LORA + d * RWKV_AAA_LORA
        lora_w = lora_w.at[lo:lo + RWKV_AAA_LORA, (2 + d) * w:(3 + d) * w].set(a_up[d])
    lo = 2 * RWKV_DECAY_LORA + 2 * RWKV_AAA_LORA
    lora_w = lora_w.at[lo:lo + RWKV_GATE_LORA, 4 * w:].set(g_up).astype(BF16)
    head = jnp.arange(w) // RWKV_HEAD_DIM
    seg = (head[:, None] == head[None, :]).astype(BF16)
    row = lambda t: t.reshape(1, -1)
    return [mu_main, mu_sm, row(k_k), row(k_a), row(r_k), w0, a0, lora_w, seg]


def _rwkv7_branch(prep, seg_len, seg, ln_w, ln_b):
    w = RWKV_WIDTH
    bsz = prep[0].shape[0]
    row = lambda t: t.reshape(1, -1)
    pr = [(a, w, 0) for a in prep]
    state = [pltpu.VMEM((RWKV_PAIRS * bsz, LANE, LANE), F32)]
    (y_bwd,) = _chunk_call_all(functools.partial(_rwkv_body, rev=True, last=False),
                               [pr[0], pr[6], pr[1], pr[2], pr[7], pr[8]], [], [w], [F32],
                               seg=seg_len, chunk=RWKV_CHUNK, rev=True, scratch=state, name="rwkv_bwd")
    ins = [pr[0], pr[3], pr[1], pr[2], pr[4], pr[5], (y_bwd, w, 0), pr[9], pr[10]]
    (out,) = _chunk_call_all(functools.partial(_rwkv_body, rev=False, last=True), ins,
                             [row(ln_w), row(ln_b), seg], [w], [BF16],
                             seg=seg_len, chunk=RWKV_CHUNK, rev=False, scratch=state, name="rwkv_fwd")
    return out


def _cd_proj_body(ins, mod_ref, sh, scr, halo, *, si):
    h = ins[0][0]
    rows = h.shape[0]
    shift, scale, _ = _mod_rows(mod_ref, si)
    w_qkv, w_z, w_gsm, w_rkv, w_wsm, conv_w, a_neg, dt_bias = sh[:8]
    xb = (h * (1.0 + scale) + shift).astype(BF16)
    if halo is None:
        xb_ext = xb
    else:
        prev_r, next_r, has_prev, has_next = halo
        h_ext = jnp.concatenate([prev_r[0][0], h, next_r[0][0]], 0)
        xb_ext = (h_ext * (1.0 + scale) + shift).astype(BF16)

    def with_halo(w):
        p = _dot(xb_ext, w[...])
        if halo is None:
            zero = jnp.zeros((SUBLANE, p.shape[1]), F32)
            return p, (zero, zero)
        return (p[SUBLANE:SUBLANE + rows],
                (jnp.where(has_prev, p[:SUBLANE], 0.0), jnp.where(has_next, p[SUBLANE + rows:], 0.0)))

    qkv, halo_qkv = with_halo(w_qkv)
    rkv, halo_rkv = with_halo(w_rkv)
    wsm, halo_wsm = with_halo(w_wsm)
    z = _dot(xb, w_z[...])
    gsm = _dot(xb, w_gsm[...])
    return ([z] + _gdn_prep(qkv, gsm, *halo_qkv, conv_w, a_neg, dt_bias)
            + _rwkv_prep(rkv, wsm, halo_rkv, halo_wsm, sh[8:]))


def _pad_cols(w, n):
    return jnp.pad(w, ((0, 0), (0, n - w.shape[1])))


def _mixer_cd(h_lat, h_ctx, mod, w_in, w_out, conv_w, a_log, dt_bias, norm_w, mu, w0, w_up, a0, a_up, g_up,
              k_k, k_a, r_k, ln_w, ln_b):
    n_qkv, n_gsm = 3 * GDN_WIDTH, 4 * GDN_HEADS
    n_rkv = 3 * RWKV_WIDTH
    lo_z, lo_g = n_qkv, n_qkv + GDN_WIDTH
    lo_r = lo_g + n_gsm
    lo_s = lo_r + n_rkv
    n_gpad, n_spad = LANE, 2 * LANE
    ws = [w_in[:, :n_qkv], w_in[:, lo_z:lo_g], _pad_cols(w_in[:, lo_g:lo_r], n_gpad),
          w_in[:, lo_r:lo_s], _pad_cols(w_in[:, lo_s:], n_spad)]
    seg = (h_lat.shape[1], h_ctx.shape[1])
    rw = _rwkv_params(n_spad, mu, w0, w_up, a0, a_up, g_up, k_k, k_a, r_k)
    shared = [w.astype(BF16) for w in ws] + [conv_w] + list(_gdn_params(a_log, dt_bias, n_gpad)) + rw
    gw, rww = GDN_WIDTH, RWKV_WIDTH
    widths = [gw, gw, gw, gw, n_gpad] + [rww] * 11
    outs = _token_call(functools.partial(_cd_proj_body, si=3), [h_lat], [h_ctx], shared, widths,
                       [F32] * len(widths), mod=mod, with_ctx=True, name="mixer_in_proj_prep",
                       n_halo=1, joint_out=True)
    d = _deltanet_branch(outs[1:5], outs[0], seg, norm_w)
    r = _rwkv7_branch(outs[5:], seg, rw[-1], ln_w, ln_b)
    return [d, r], [d, r], [w_out[:GDN_WIDTH].astype(BF16), w_out[GDN_WIDTH:].astype(BF16)]


def kernel(x, c, ctx, c_ctx, ada_w, ada_b, ffn_w_in, ffn_w_out, ln_g, ln_b, ab_w_in, ab_w_out, s5_lam_re, s5_lam_im, s5_log_dt, s5_b_re, s5_b_im, s5_c_re, s5_c_im, s5_d, s5_glu_w, s5_glu_b, ret_log_rate, cd_w_in, cd_w_out, gdn_conv_w, gdn_a_log, gdn_dt_bias, gdn_norm_w, rwkv_mu, rwkv_w0, rwkv_w_up, rwkv_a0, rwkv_a_up, rwkv_g_up, rwkv_k_k, rwkv_k_a, rwkv_r_k, rwkv_ln_w, rwkv_ln_b):
    mods = _adaln(c, c_ctx, ada_w, ada_b)
    h_lat, h_ctx = x, ctx
    for i in range(DEPTH):
        mod = mods[i]
        j = i // 2
        keep_ctx = i < DEPTH - 1
        h_lat, h_ctx = _ffn(h_lat, h_ctx, mod, 0, ffn_w_in[i, 0].astype(BF16), ffn_w_out[i, 0].astype(BF16),
                            ln_g[i, 0], ln_b[i, 0])
        if i % 2 == 0:
            s5_ops = _s5_operators(s5_lam_re[j], s5_lam_im[j], s5_log_dt[j], s5_b_re[j], s5_b_im[j],
                                   s5_c_re[j], s5_c_im[j])
            ms_lat, ms_ctx, w_out = _mixer_ab(h_lat, h_ctx, mod, ab_w_in[j], ab_w_out[j], s5_ops, s5_d[j],
                                              s5_glu_w[j], s5_glu_b[j], ret_log_rate[j])
        else:
            ms_lat, ms_ctx, w_out = _mixer_cd(
                h_lat, h_ctx, mod, cd_w_in[j], cd_w_out[j], gdn_conv_w[j], gdn_a_log[j], gdn_dt_bias[j],
                gdn_norm_w[j], rwkv_mu[j], rwkv_w0[j], rwkv_w_up[j], rwkv_a0[j], rwkv_a_up[j], rwkv_g_up[j],
                rwkv_k_k[j], rwkv_k_a[j], rwkv_r_k[j], rwkv_ln_w[j], rwkv_ln_b[j])
        h_lat, h_ctx = _ffn(h_lat, h_ctx, mod, 6, ffn_w_in[i, 1].astype(BF16), ffn_w_out[i, 1].astype(BF16),
                            ln_g[i, 2], ln_b[i, 2], with_ctx=keep_ctx,
                            mix=(ms_lat, ms_ctx, w_out, 5, ln_g[i, 1], ln_b[i, 1]))
    return h_lat
```

```python
import functools
import math

import jax
import jax.numpy as jnp
from jax import lax
from jax.experimental import pallas as pl
from jax.experimental.pallas import tpu as pltpu

F32 = jnp.float32
BF16 = jnp.bfloat16
HI = lax.Precision.HIGHEST

D_MODEL = 1024
DEPTH = 4
N_MOD = 9
D_FF = 2816
MACARON = 0.5
ALPHA = (2.0 * DEPTH) ** 0.25
LN_EPS = 1e-5
GRID_W = 64

LANE = 128
SUBLANE = 8
MXU_TILE = 256
VMEM_LIMIT_BYTES = 56 * 1024 * 1024

TOKEN_ROWS = 512
FF_CHUNK = MXU_TILE

def _cparams(*sem):
    return pltpu.CompilerParams(dimension_semantics=sem, vmem_limit_bytes=VMEM_LIMIT_BYTES)


def _full_spec(a):
    nd = a.ndim
    return pl.BlockSpec(a.shape, lambda *_, nd=nd: (0,) * nd, pipeline_mode=pl.Buffered(1))


def _dot(a, b):
    return jnp.dot(a, b, preferred_element_type=F32)


def _dot_nt(a, b):
    return lax.dot_general(a, b, (((1,), (1,)), ((), ())), preferred_element_type=F32)


def _dot_tn(a, b):
    return lax.dot_general(a, b, (((0,), (0,)), ((), ())), preferred_element_type=F32)


def _layer_norm(x, g, b):
    mu = jnp.mean(x, -1, keepdims=True)
    xc = x - mu
    var = jnp.mean(xc * xc, -1, keepdims=True)
    return xc * lax.rsqrt(var + LN_EPS) * g + b


def _adaln_kernel(s_ref, w_ref, b_ref, o_ref):
    s = s_ref[...]
    s = s * jax.nn.sigmoid(s)
    o_ref[0] = jnp.dot(s, w_ref[0], preferred_element_type=F32, precision=HI) + b_ref[0]


def _adaln(c, c_ctx, ada_w, ada_b):
    bsz = c.shape[0]
    rows = 2 * SUBLANE
    s = jnp.zeros((rows, D_MODEL), F32).at[:bsz].set(c).at[bsz].set(c_ctx)
    n_out = N_MOD * D_MODEL
    tn = D_MODEL
    out = pl.pallas_call(
        _adaln_kernel,
        grid=(DEPTH, n_out // tn),
        in_specs=[
            pl.BlockSpec((rows, D_MODEL), lambda i, j: (0, 0)),
            pl.BlockSpec((1, D_MODEL, tn), lambda i, j: (i, 0, j)),
            pl.BlockSpec((1, 1, tn), lambda i, j: (i, 0, j)),
        ],
        out_specs=pl.BlockSpec((1, rows, tn), lambda i, j: (i, 0, j)),
        out_shape=jax.ShapeDtypeStruct((DEPTH, rows, n_out), F32),
        compiler_params=_cparams("arbitrary", "arbitrary"),
        name="adaln",
    )(s, ada_w, ada_b.reshape(DEPTH, 1, n_out))
    return out.reshape(DEPTH, rows, N_MOD, D_MODEL)


def _token_call(body, lat_ins, ctx_ins, shared, out_widths, out_dtypes, *, mod, with_ctx, name,
                scratch_fn=None, n_halo=0, seg=None, joint_out=False):
    bsz = lat_ins[0].shape[0]
    if seg is None:
        seg = (lat_ins[0].shape[1], ctx_ins[0].shape[1] if with_ctx else 0)
    n_lat, n_ctx = seg
    tm = n_ctx if joint_out else TOKEN_ROWS
    nlb = n_lat // tm
    n_in, n_sh, n_out = len(lat_ins), len(shared), len(out_widths)
    sub_per_blk = tm // SUBLANE
    joint_in = [with_ctx and lat_ins[i] is ctx_ins[i] for i in range(n_in)]

    def lat_map(b, j):
        return (b, jnp.minimum(j, nlb - 1), 0)

    def ctx_map(b, j):
        return (b, 0, 0)

    def joint_ctx_map(b, j):
        return (b, n_lat // n_ctx, 0)

    def joint_out_map(b, j):
        return (b, j, 0)

    def prev_map(b, j):
        return (b, jnp.maximum(jnp.minimum(j, nlb - 1) * sub_per_blk - 1, 0), 0)

    def next_map(b, j):
        return (b, jnp.minimum((jnp.minimum(j, nlb - 1) + 1) * sub_per_blk, n_lat // SUBLANE - 1), 0)

    def per_batch(a, index_map):
        if a.shape[0] == bsz:
            return index_map
        return lambda b, j: (0,) + index_map(b, j)[1:]

    in_specs = [pl.BlockSpec((1, tm, a.shape[2]), per_batch(a, lat_map)) for a in lat_ins]
    args = list(lat_ins)
    for a in lat_ins[:n_halo]:
        in_specs += [pl.BlockSpec((1, SUBLANE, a.shape[2]), prev_map),
                     pl.BlockSpec((1, SUBLANE, a.shape[2]), next_map)]
        args += [a, a]
    if with_ctx:
        in_specs += [pl.BlockSpec((1, n_ctx, a.shape[2]), per_batch(a, joint_ctx_map if jt else ctx_map))
                     for a, jt in zip(ctx_ins, joint_in)]
        args += list(ctx_ins)
    n_mod = 0
    if mod is not None:
        in_specs.append(pl.BlockSpec((1, N_MOD, D_MODEL), lambda b, j: (b, 0, 0)))
        args.append(mod)
        n_mod = 1
        if with_ctx:
            in_specs.append(pl.BlockSpec((1, N_MOD, D_MODEL), lambda b, j: (bsz, 0, 0)))
            args.append(mod)
            n_mod = 2
    in_specs += [_full_spec(s) for s in shared]
    args += list(shared)

    if joint_out:
        out_specs = [pl.BlockSpec((1, tm, w), joint_out_map) for w in out_widths]
        out_shape = [jax.ShapeDtypeStruct((bsz, n_lat + n_ctx, w), dt) for w, dt in zip(out_widths, out_dtypes)]
    else:
        out_specs = [pl.BlockSpec((1, tm, w), lat_map) for w in out_widths]
        out_shape = [jax.ShapeDtypeStruct((bsz, n_lat, w), dt) for w, dt in zip(out_widths, out_dtypes)]
        if with_ctx:
            out_specs += [pl.BlockSpec((1, n_ctx, w), ctx_map) for w in out_widths]
            out_shape += [jax.ShapeDtypeStruct((bsz, n_ctx, w), dt) for w, dt in zip(out_widths, out_dtypes)]
    scratch = scratch_fn(tm) if scratch_fn is not None else []

    def kern(*refs):
        pos = 0
        lat_r = refs[pos:pos + n_in]; pos += n_in
        halo_r = refs[pos:pos + 2 * n_halo]; pos += 2 * n_halo
        ctx_r = ()
        if with_ctx:
            ctx_r = refs[pos:pos + n_in]; pos += n_in
        mod_r = refs[pos:pos + n_mod]; pos += n_mod
        sh_r = refs[pos:pos + n_sh]; pos += n_sh
        lat_o = refs[pos:pos + n_out]; pos += n_out
        ctx_o = lat_o
        if with_ctx and not joint_out:
            ctx_o = refs[pos:pos + n_out]; pos += n_out
        scr = refs[pos:]
        j = pl.program_id(1)

        def run(ins, modr, outs, halo):
            extra = (halo,) if n_halo else ()
            vals = body(ins, modr, sh_r, scr, *extra)
            for o, v in zip(outs, vals):
                o[0] = v.astype(o.dtype)

        lat_halo = (halo_r[0::2], halo_r[1::2], j > 0, j < nlb - 1) if n_halo else None
        if with_ctx:

            @pl.when(j < nlb)
            def _():
                run(lat_r, mod_r[0] if n_mod else None, lat_o, lat_halo)

            @pl.when(j == nlb)
            def _():
                run(ctx_r, mod_r[1] if n_mod else None, ctx_o, None)
        else:
            run(lat_r, mod_r[0] if n_mod else None, lat_o, lat_halo)

    outs = pl.pallas_call(
        kern,
        grid=(bsz, nlb + (1 if with_ctx else 0)),
        in_specs=in_specs,
        out_specs=out_specs,
        out_shape=out_shape,
        scratch_shapes=scratch,
        compiler_params=_cparams("arbitrary", "arbitrary"),
        name=name,
    )(*args)
    if joint_out:
        return list(outs)
    lat_out = list(outs[:n_out])
    ctx_out = list(outs[n_out:]) if with_ctx else [None] * n_out
    return lat_out, ctx_out


def _mod_rows(mod_ref, i):
    return mod_ref[0, i:i + 1, :], mod_ref[0, i + 1:i + 2, :], mod_ref[0, i + 2:i + 3, :]


def _ffn_body(ins, mod_ref, sh, scr, *, si, n_mix=0, gi=None):
    h = ins[0][0]
    if n_mix:
        gate = mod_ref[0, gi:gi + 1, :]
        y = _dot(ins[1][0], sh[0][...])
        for k in range(1, n_mix):
            y = y + _dot(ins[1 + k][0], sh[k][...])
        h = _layer_norm(ALPHA * h + gate * y, sh[n_mix][...], sh[n_mix + 1][...])
        sh = sh[n_mix + 2:]
    w_in, w_out, g, b = sh
    act = scr[0]
    shift, scale, gate = _mod_rows(mod_ref, si)
    rows = h.shape[0]
    xb = (h * (1.0 + scale) + shift).astype(BF16)
    for j in range(D_FF // FF_CHUNK):
        lo = j * FF_CHUNK
        a = _dot(xb, w_in[:, lo:lo + FF_CHUNK])
        bb = _dot(xb, w_in[:, D_FF + lo:D_FF + lo + FF_CHUNK])
        act[0:rows, lo:lo + FF_CHUNK] = (a * jax.nn.sigmoid(a) * bb).astype(BF16)
    y = _dot(act[0:rows, :], w_out[...])
    return [_layer_norm(ALPHA * h + gate * (MACARON * y), g[...], b[...])]


def _ffn(h_lat, h_ctx, mod, si, w_in, w_out, g, b, with_ctx=True, mix=None):
    row = lambda t: t.reshape(1, -1)
    lat_ins, ctx_ins, shared, kw = [h_lat], [h_ctx], [], {}
    if mix is not None:
        ms_lat, ms_ctx, ws, gi, g_mix, b_mix = mix
        lat_ins, ctx_ins = lat_ins + list(ms_lat), ctx_ins + list(ms_ctx)
        shared = list(ws) + [row(g_mix), row(b_mix)]
        kw = dict(n_mix=len(ws), gi=gi)
    seg = (h_lat.shape[1], h_ctx.shape[1] if with_ctx else 0)
    lat, ctx = _token_call(
        functools.partial(_ffn_body, si=si, **kw), lat_ins, ctx_ins, shared + [w_in, w_out, row(g), row(b)],
        [D_MODEL], [F32], mod=mod, with_ctx=with_ctx, name="ffn_half_step", seg=seg,
        scratch_fn=lambda tm: [pltpu.VMEM((tm, D_FF), BF16)])
    return lat[0], ctx[0]


def _chunk_call_all(body, ins, shared, out_widths, out_dtypes, *, seg, chunk, rev, scratch, name,
                    per_batch=False):
    n_lat, n_ctx = seg
    bsz = ins[0][0].shape[0]
    ncl, ncc = n_lat // chunk, n_ctx // chunk
    n_all = ncl + ncc
    rows = 1 if per_batch else bsz

    def blk(n):
        if rev:
            return n_all - 1 - n
        return jnp.where(n < ncc, ncl + n, n - ncc)

    def index(cb):
        if per_batch:
            return lambda b, n: (b, blk(n), cb)
        return lambda n: (0, blk(n), cb)

    in_specs = [pl.BlockSpec((rows, chunk, w), index(cb)) for _, w, cb in ins]
    in_specs += [_full_spec(s) for s in shared]
    out_specs = [pl.BlockSpec((rows, chunk, w), index(0)) for w in out_widths]
    out_shape = [jax.ShapeDtypeStruct((bsz, n_lat + n_ctx, w), dt) for w, dt in zip(out_widths, out_dtypes)]
    n_i, n_s, n_o = len(ins), len(shared), len(out_widths)

    def kern(*refs):
        in_r, sh = refs[:n_i], refs[n_i:n_i + n_s]
        out_r = refs[n_i + n_s:n_i + n_s + n_o]
        scr = refs[n_i + n_s + n_o:]
        outs = body(pl.program_id(1 if per_batch else 0), [r[...] for r in in_r], sh, scr)
        for o, v in zip(out_r, outs):
            o[...] = v.astype(o.dtype)

    return pl.pallas_call(
        kern,
        grid=(bsz, n_all) if per_batch else (n_all,),
        in_specs=in_specs,
        out_specs=out_specs,
        out_shape=out_shape,
        scratch_shapes=scratch,
        compiler_params=_cparams(*(["arbitrary"] * (2 if per_batch else 1))),
        name=name,
    )(*[a for a, _, _ in ins], *shared)


S5_WIDTH = 256
S5_GROUP = 16
S5_GROUPS = 16
S5_STATE = 64
S5_T = 4
S5_ROW = S5_T * S5_WIDTH
S5_NSTATE = S5_GROUPS * S5_STATE


def _s5_operators(lam_re, lam_im, log_dt, b_re, b_im, c_re, c_im):
    t_len, g_n, p_n, c_n = S5_T, S5_GROUPS, S5_STATE, S5_GROUP
    dt = jnp.exp(log_dt)[..., None]
    a_r, a_i = lam_re * dt, lam_im * dt
    mag = jnp.exp(a_r)
    lb_re, lb_im = mag * jnp.cos(a_i), mag * jnp.sin(a_i)
    den = lam_re * lam_re + lam_im * lam_im
    nr = lb_re - 1.0
    coef_re = (nr * lam_re + lb_im * lam_im) / den
    coef_im = (lb_im * lam_re - nr * lam_im) / den
    bb_re = coef_re[..., None] * b_re - coef_im[..., None] * b_im
    bb_im = coef_re[..., None] * b_im + coef_im[..., None] * b_re

    def lam_pow(e):
        m = jnp.exp(a_r * e)
        return m * jnp.cos(a_i * e), m * jnp.sin(a_i * e)

    eye = jnp.eye(g_n, dtype=F32)

    def kern_tau(d, tau):
        pr, pi = lam_pow(float(tau))
        cp_re = c_re * pr[d][:, None, :] - c_im * pi[d][:, None, :]
        cp_im = c_re * pi[d][:, None, :] + c_im * pr[d][:, None, :]
        k = (jnp.einsum("gcp,gpe->gec", cp_re, bb_re[d], precision=HI)
             - jnp.einsum("gcp,gpe->gec", cp_im, bb_im[d], precision=HI))
        return k

    k0 = [kern_tau(0, tau) for tau in range(t_len)]
    k1 = [kern_tau(1, tau) for tau in range(t_len)]
    zero = jnp.zeros_like(k0[0])
    rows = []
    for s in range(t_len):
        cols = []
        for t in range(t_len):
            blk = zero
            if s <= t:
                blk = blk + k0[t - s]
            if s >= t:
                blk = blk + k1[s - t]
            cols.append(blk)
        rows.append(jnp.stack(cols, 0))
    kst = jnp.stack(rows, 0)
    m_op = jnp.einsum("stgec,gh->sgethc", kst, eye).reshape(S5_ROW, S5_ROW)

    w_ops, v_ops, lam_t = [], [], []
    for d in range(2):
        w_re, w_im, v_re, v_im = [], [], [], []
        for s in range(t_len):
            pr, pi = lam_pow(float(t_len - 1 - s) if d == 0 else float(s))
            w_re.append(pr[d][..., None] * bb_re[d] - pi[d][..., None] * bb_im[d])
            w_im.append(pr[d][..., None] * bb_im[d] + pi[d][..., None] * bb_re[d])
        for t in range(t_len):
            pr, pi = lam_pow(float(t + 1) if d == 0 else float(t_len - t))
            v_re.append(c_re * pr[d][:, None, :] - c_im * pi[d][:, None, :])
            v_im.append(-(c_re * pi[d][:, None, :] + c_im * pr[d][:, None, :]))
        for w in (w_re, w_im):
            w_ops.append(jnp.einsum("sgpe,gh->sgehp", jnp.stack(w, 0), eye).reshape(S5_ROW, S5_NSTATE))
        for v in (v_re, v_im):
            v_ops.append(jnp.einsum("tgcp,gh->gpthc", jnp.stack(v, 0), eye).reshape(S5_NSTATE, S5_ROW))
        pr, pi = lam_pow(float(t_len))
        lam_t.append((pr[d].reshape(1, S5_NSTATE), pi[d].reshape(1, S5_NSTATE)))
    w_cat = jnp.concatenate([m_op] + w_ops, axis=1).astype(BF16)
    return w_cat, [v.astype(BF16) for v in v_ops], lam_t


def _s5_drive_body(ins, mod_ref, sh, scr):
    xb = ins[0][0].astype(BF16)
    w = sh[0]
    n = w.shape[1] // S5_ROW
    return [_dot(xb, w[:, i * S5_ROW:(i + 1) * S5_ROW]) for i in range(n)]


S5_SCAN_LANES = 8 * LANE


def _complex_powers(lam, n):
    re, im = [jnp.ones_like(lam[0])], [jnp.zeros_like(lam[0])]
    for _ in range(n):
        re, im = re + [re[-1] * lam[0] - im[-1] * lam[1]], im + [re[-1] * lam[1] + im[-1] * lam[0]]
    return jnp.concatenate(re, 0), jnp.concatenate(im, 0)


def _s5_state_scan(b_re, b_im, lam, rev):
    (bre_l, bre_c), (bim_l, bim_c) = b_re, b_im
    bsz, nl, w = bre_l.shape
    nc = bre_c.shape[1]
    r = SUBLANE
    wl = S5_SCAN_LANES
    pw_re, pw_im = _complex_powers(lam, r)
    et_re, et_im = (pw_re[r - 1::-1], pw_im[r - 1::-1]) if rev else (pw_re[:r], pw_im[:r])

    def kern(brl, bil, brc, bic, pwr, pwi, etr, eti, hrl, hil, hrc, hic):
        row = lax.broadcasted_iota(jnp.int32, (r, wl), 0)

        def shifted(y, k):
            if rev:
                return jnp.where(row < r - k, pltpu.roll(y, r - k, 0), 0.0)
            return jnp.where(row >= k, pltpu.roll(y, k, 0), 0.0)

        steps = [(k, pwr[k:k + 1, :], pwi[k:k + 1, :]) for k in (1, 2, 4)]
        er, ei = etr[...], eti[...]
        l8r, l8i = pwr[r:r + 1, :], pwi[r:r + 1, :]
        last = 0 if rev else r - 1

        def make(br, bi, hr_o, hi_o, nblk):
            def step(i, carry):
                cr, ci = carry
                blk = (nblk - 1 - i) if rev else i
                idx = pl.ds(pl.multiple_of(blk * r, r), r)
                yr, yi = br[0, idx, :], bi[0, idx, :]
                for k, pr, pi in steps:
                    sr, si = shifted(yr, k), shifted(yi, k)
                    yr, yi = yr + pr * sr - pi * si, yi + pr * si + pi * sr
                hr_o[0, idx, :] = er * cr - ei * ci + shifted(yr, 1)
                hi_o[0, idx, :] = er * ci + ei * cr + shifted(yi, 1)
                return (l8r * cr - l8i * ci + yr[last:last + 1, :], l8r * ci + l8i * cr + yi[last:last + 1, :])
            return step

        z = jnp.zeros((1, wl), F32)
        carry = lax.fori_loop(0, nc // r, make(brc, bic, hrc, hic, nc // r), (z, z))
        lax.fori_loop(0, nl // r, make(brl, bil, hrl, hil, nl // r), carry, unroll=8)

    def spec(rows):
        return pl.BlockSpec((1, rows, wl), lambda b, j: (b, 0, j))

    def tab(rows):
        return pl.BlockSpec((rows, wl), lambda b, j: (0, j))

    outs = pl.pallas_call(
        kern,
        grid=(bsz, w // wl),
        in_specs=[spec(nl), spec(nl), spec(nc), spec(nc), tab(r + 1), tab(r + 1), tab(r), tab(r)],
        out_specs=[spec(nl), spec(nl), spec(nc), spec(nc)],
        out_shape=[jax.ShapeDtypeStruct((bsz, nl, w), F32), jax.ShapeDtypeStruct((bsz, nl, w), F32),
                   jax.ShapeDtypeStruct((bsz, nc, w), F32), jax.ShapeDtypeStruct((bsz, nc, w), F32)],
        compiler_params=_cparams("arbitrary", "arbitrary"),
        name="s5_state_scan",
    )(bre_l, bim_l, bre_c, bim_c, pw_re, pw_im, et_re, et_im)
    hrl, hil, hrc, hic = outs
    return (hrl, hrc), (hil, hic)


def _s5_finish_body(ins, mod_ref, sh, scr):
    yi, h0r, h0i, h1r, h1i, u = [r[0] for r in ins]
    v0r, v0i, v1r, v1i, d_skip, glu_w, glu_b = sh
    y = yi + d_skip[...] * u
    for h, v in ((h0r, v0r), (h0i, v0i), (h1r, v1r), (h1i, v1i)):
        y = y + _dot(h.astype(BF16), v[...])
    z = jax.nn.gelu(y)
    return [z * jax.nn.sigmoid(_dot(z.astype(BF16), glu_w[...]) + glu_b[...])]


def _s5_branch(u, seg, ops, d_skip, glu_w, glu_b):
    w_cat, v_ops, lam_t = ops
    bsz = u.shape[0]
    n_lat, n_ctx = seg
    seg4 = (n_lat // S5_T, n_ctx // S5_T)
    u4l = u4c = u.reshape(bsz, seg4[0] + seg4[1], S5_ROW)
    lat, ctx = _token_call(_s5_drive_body, [u4l], [u4c], [w_cat], [S5_ROW] * 5, [F32] * 5,
                           mod=None, with_ctx=True, name="s5_drive", seg=seg4)
    pairs = list(zip(lat, ctx))
    h0 = _s5_state_scan(pairs[1], pairs[2], lam_t[0], rev=False)
    h1 = _s5_state_scan(pairs[3], pairs[4], lam_t[1], rev=True)
    eye_t = jnp.eye(S5_T, dtype=F32)
    glu_k = jnp.kron(eye_t, glu_w).astype(BF16)
    fin_l = [pairs[0][0], h0[0][0], h0[1][0], h1[0][0], h1[1][0], u4l]
    fin_c = [pairs[0][1], h0[0][1], h0[1][1], h1[0][1], h1[1][1], u4c]
    shared = list(v_ops) + [jnp.tile(d_skip, S5_T).reshape(1, S5_ROW), glu_k, jnp.tile(glu_b, S5_T).reshape(1, S5_ROW)]
    lat, ctx = _token_call(_s5_finish_body, fin_l, fin_c, shared, [S5_ROW], [BF16],
                           mod=None, with_ctx=True, name="s5_finish", seg=seg4)
    return lat[0].reshape(bsz, n_lat, S5_WIDTH), ctx[0].reshape(bsz, n_ctx, S5_WIDTH)


RET_HEADS = 6
RET_HEAD_DIM = 128
RET_WIDTH = RET_HEADS * RET_HEAD_DIM
RET_CHUNK = 256
ROPE_BASE = 10000.0


def _rope_tables(n_lat):
    nf = RET_HEAD_DIM // 4
    rows = n_lat // GRID_W
    freqs = ROPE_BASE ** (-jnp.arange(nf, dtype=F32) / nf)
    pr = jnp.broadcast_to(jnp.arange(rows, dtype=F32)[:, None], (rows, GRID_W)).reshape(-1)
    pc = jnp.broadcast_to(jnp.arange(GRID_W, dtype=F32)[None, :], (rows, GRID_W)).reshape(-1)
    ang = jnp.concatenate([pr[:, None] * freqs, pc[:, None] * freqs], -1)
    cos, sin = jnp.cos(ang), jnp.sin(ang)
    return jnp.concatenate([cos, cos], -1), jnp.concatenate([-sin, sin], -1)


def _ret_tables(log_rate):
    c = RET_CHUNK
    lg = -jnp.exp(log_rate.astype(F32))
    idx = jnp.arange(c, dtype=F32)
    diff = idx[:, None] - idx[None, :]
    past, fut = diff >= 0, diff <= 0
    d0 = jnp.where(past, jnp.exp(jnp.where(past, diff, 0.0)[None] * lg[0][:, None, None]), 0.0)
    d1 = jnp.where(fut, jnp.exp(jnp.where(fut, -diff, 0.0)[None] * lg[1][:, None, None]), 0.0)

    def rep(t):
        return jnp.repeat(t, RET_HEAD_DIM, axis=-1)

    fwd = (rep(jnp.exp((idx + 1.0)[:, None] * lg[0])), rep(jnp.exp((c - 1.0 - idx)[:, None] * lg[0])),
           rep(jnp.exp(c * lg[0])[None]))
    bwd = (rep(jnp.exp((c - idx)[:, None] * lg[1])), rep(jnp.exp(idx[:, None] * lg[1])),
           rep(jnp.exp(c * lg[1])[None]))
    return d0 + d1, fwd, bwd


def _rope(x, cos, sin):
    return x * cos + pltpu.roll(x, RET_HEAD_DIM // 2, 1) * sin


def _proj_rope_body(ins, mod_ref, sh, scr, *, si):
    h, cos, sin = ins[0][0], ins[1][0], ins[2][0]
    shift, scale, _ = _mod_rows(mod_ref, si)
    xb = (h * (1.0 + scale) + shift).astype(BF16)

    def rotated(w, mult):
        y = _dot(xb, w[...])
        cols = [_rope(y[:, i * RET_HEAD_DIM:(i + 1) * RET_HEAD_DIM], cos, sin) * mult for i in range(RET_HEADS)]
        return jnp.concatenate(cols, -1)

    return [rotated(sh[0], 1.0), rotated(sh[1], RET_HEAD_DIM ** -0.5)] + [_dot(xb, w[...]) for w in sh[2:]]


def _ret_bwd_body(n, vals, sh, scr):
    q, k, v = (t[0] for t in vals)
    xi, zeta, gc = sh
    state = scr[0]

    @pl.when(n == 0)
    def _():
        state[...] = jnp.zeros_like(state)

    outs = []
    for h in range(RET_HEADS):
        sl = slice(h * RET_HEAD_DIM, (h + 1) * RET_HEAD_DIM)
        s_old = state[h]
        outs.append(_dot((q[:, sl] * xi[:, sl]).astype(BF16), s_old.astype(BF16)))
        state[h] = gc[:, sl] * s_old + _dot_tn((k[:, sl] * zeta[:, sl]).astype(BF16), v[:, sl].astype(BF16))
    return [jnp.concatenate(outs, -1)[None]]


def _ret_fwd_body(n, vals, sh, scr):
    q, k, v, g, o_bwd = (t[0] for t in vals)
    dcomb, xi, zeta, gc = sh
    state = scr[0]

    @pl.when(n == 0)
    def _():
        state[...] = jnp.zeros_like(state)

    outs = []
    for h in range(RET_HEADS):
        sl = slice(h * RET_HEAD_DIM, (h + 1) * RET_HEAD_DIM)
        qh, kh = q[:, sl], k[:, sl]
        vb = v[:, sl].astype(BF16)
        s_old = state[h]
        scores = _dot_nt(qh.astype(BF16), kh.astype(BF16)) * dcomb[h]
        o = (_dot(scores.astype(BF16), vb) + _dot((qh * xi[:, sl]).astype(BF16), s_old.astype(BF16))
             + o_bwd[:, sl])
        state[h] = gc[:, sl] * s_old + _dot_tn((kh * zeta[:, sl]).astype(BF16), vb)
        mu = jnp.mean(o, -1, keepdims=True)
        oc = o - mu
        var = jnp.mean(oc * oc, -1, keepdims=True)
        gh = g[:, sl]
        outs.append(gh * jax.nn.sigmoid(gh) * (oc * lax.rsqrt(var + LN_EPS)))
    return [jnp.concatenate(outs, -1)[None]]


def _retention_branch(q, k, vg, seg, log_rate):
    dcomb, fwd, bwd = _ret_tables(log_rate)
    w = RET_WIDTH
    state = [pltpu.VMEM((RET_HEADS, RET_HEAD_DIM, RET_HEAD_DIM), F32)]
    qkv = [(q, w, 0), (k, w, 0), (vg, w, 0)]
    (o_bwd,) = _chunk_call_all(_ret_bwd_body, qkv, list(bwd), [w], [F32], seg=seg, chunk=RET_CHUNK,
                               rev=True, scratch=state, name="retention_bwd", per_batch=True)
    ins = qkv + [(vg, w, 1), (o_bwd, w, 0)]
    (r,) = _chunk_call_all(_ret_fwd_body, ins, [dcomb] + list(fwd), [w], [BF16], seg=seg, chunk=RET_CHUNK,
                           rev=False, scratch=state, name="retention_fwd", per_batch=True)
    return r


def _mixer_ab(h_lat, h_ctx, mod, w_in, w_out, s5_ops, s5_d, s5_glu_w, s5_glu_b, ret_log_rate):
    bsz, n_lat, _ = h_lat.shape
    n_ctx = h_ctx.shape[1]
    w = RET_WIDTH
    lo_q, lo_k, lo_v = S5_WIDTH, S5_WIDTH + w, S5_WIDTH + 2 * w
    ws = [w_in[:, lo_q:lo_k], w_in[:, lo_k:lo_v], w_in[:, lo_v:], w_in[:, :S5_WIDTH]]
    cos, sin = _rope_tables(n_lat)
    lat_ins = [h_lat, cos[None], sin[None]]
    ctx_ins = [h_ctx, jnp.ones((1, n_ctx, RET_HEAD_DIM), F32), jnp.zeros((1, n_ctx, RET_HEAD_DIM), F32)]
    q, k, vg, u = _token_call(functools.partial(_proj_rope_body, si=3), lat_ins, ctx_ins,
                              [t.astype(BF16) for t in ws], [w, w, 2 * w, S5_WIDTH], [F32] * 4,
                              mod=mod, with_ctx=True, name="mixer_in_proj", joint_out=True)
    a_lat, a_ctx = _s5_branch(u, (n_lat, n_ctx), s5_ops, s5_d, s5_glu_w, s5_glu_b)
    r = _retention_branch(q, k, vg, (n_lat, n_ctx), ret_log_rate)
    return [a_lat, r], [a_ctx, r], [w_out[:S5_WIDTH].astype(BF16), w_out[S5_WIDTH:].astype(BF16)]


def _mm(a, b, exact=False):
    if exact:
        return jnp.dot(a, b, preferred_element_type=F32, precision=HI)
    return _dot(a.astype(BF16), b.astype(BF16))


def _softplus(x):
    return jnp.maximum(x, 0.0) + jnp.log1p(jnp.exp(-jnp.abs(x)))


def _shift_rows(x, s, prev8, next8):
    rows = x.shape[0]
    rolled = pltpu.roll(x, (-s) % rows, 0)
    r8 = lax.broadcasted_iota(jnp.int32, (SUBLANE, x.shape[1]), 0)
    if s < 0:
        edge = jnp.where(r8 < -s, pltpu.roll(prev8, -s, 0), rolled[:SUBLANE])
        return jnp.concatenate([edge, rolled[SUBLANE:]], 0)
    edge = jnp.where(r8 >= SUBLANE - s, pltpu.roll(next8, SUBLANE - s, 0), rolled[rows - SUBLANE:])
    return jnp.concatenate([rolled[:rows - SUBLANE], edge], 0)


def _order_masks(c, rev, reps=1):
    ri = lax.broadcasted_iota(jnp.int32, (c, reps * c), 0)
    ci = jnp.bitwise_and(lax.broadcasted_iota(jnp.int32, (c, reps * c), 1), c - 1)
    if rev:
        return ri <= ci, ri < ci
    return ri >= ci, ri > ci


TRI_BASE = 8


def _bmm(a, b):
    return lax.dot_general(a.astype(BF16), b.astype(BF16), (((2,), (1,)), ((0,), (0,))),
                           preferred_element_type=F32)


def _bmm_nt(a, b):
    return lax.dot_general(a.astype(BF16), b.astype(BF16), (((2,), (2,)), ((0,), (0,))),
                           preferred_element_type=F32)


def _bmm_tn(a, b):
    return lax.dot_general(a.astype(BF16), b.astype(BF16), (((1,), (1,)), ((0,), (0,))),
                           preferred_element_type=F32)


def _tri_inverse(n_mat, mul):
    c, lanes = n_mat.shape[1], n_mat.shape[2]
    ri = lax.broadcasted_iota(jnp.int32, (c, lanes), 0)
    ci = jnp.bitwise_and(lax.broadcasted_iota(jnp.int32, (c, lanes), 1), c - 1)

    def same_block(bits):
        return jnp.right_shift(ri, bits) == jnp.right_shift(ci, bits)

    bits = TRI_BASE.bit_length() - 1
    same = same_block(bits)
    p = jnp.where(same, n_mat, 0.0)
    x = jnp.where(ri == ci, 1.0, 0.0) + p
    p = mul(p, p)
    k = 2
    while 2 * k < TRI_BASE:
        t = mul(jnp.concatenate([x, p], axis=1), p)
        x, p = x + t[:, :c], t[:, c:]
        k *= 2
    x = x + mul(x, p)
    while (1 << bits) < c:
        bits += 1
        same2 = same_block(bits)
        off = jnp.where(jnp.logical_and(same2, jnp.logical_not(same)), n_mat, 0.0)
        x = x + mul(mul(x, off), x)
        same = same2
    return x


def _heads_to_batch(x, n):
    w = x.shape[2] // n
    return jnp.concatenate([x[:, :, i * w:(i + 1) * w] for i in range(n)], axis=0)


def _batch_to_heads(x, n):
    b = x.shape[0] // n
    return jnp.concatenate([x[i * b:(i + 1) * b] for i in range(n)], axis=-1)


def _scan_cumsum(tri, x):
    bsz, _, w = x.shape
    flat = jnp.concatenate([x[b] for b in range(bsz)], axis=-1)
    cum = _mm(tri, flat, exact=True)
    return jnp.stack([cum[:, b * w:(b + 1) * w] for b in range(bsz)], axis=0)


def _seg_sum(x, seg):
    hi = x.astype(BF16)
    lo = (x - hi.astype(F32)).astype(BF16)
    return _dot(hi, seg) + _dot(lo, seg)


GDN_HEADS = 4
GDN_HEAD_DIM = 128
GDN_WIDTH = GDN_HEADS * GDN_HEAD_DIM
GDN_CONV = 5
GDN_CHUNK = 64


def _gdn_prep(qkv, sm, prev8, next8, conv_w, a_neg, dt_bias):
    half = GDN_CONV // 2
    acc = conv_w[half:half + 1, :] * qkv
    for i in range(GDN_CONV):
        if i != half:
            acc = acc + conv_w[i:i + 1, :] * _shift_rows(qkv, i - half, prev8, next8)
    y = acc * jax.nn.sigmoid(acc)
    outs = []
    for part, scale in ((0, GDN_HEAD_DIM ** -0.5), (1, 1.0)):
        cols = []
        for h in range(GDN_HEADS):
            lo = part * GDN_WIDTH + h * GDN_HEAD_DIM
            t = y[:, lo:lo + GDN_HEAD_DIM]
            cols.append(t * (lax.rsqrt(jnp.sum(t * t, -1, keepdims=True) + 1e-6) * scale))
        outs.append(jnp.concatenate(cols, -1))
    outs.append(y[:, 2 * GDN_WIDTH:])
    lane = lax.broadcasted_iota(jnp.int32, sm.shape, 1)
    g = a_neg[...] * _softplus(sm + dt_bias[...])
    outs.append(jnp.where(lane < 2 * GDN_HEADS, g, jax.nn.sigmoid(sm)))
    return outs


def _gdn_body(n, vals, sh, scr, *, rev, last):
    q, k, v, sm = vals[:4]
    state = scr[0]
    c = GDN_CHUNK
    bsz = q.shape[0]

    @pl.when(n == 0)
    def _():
        state[...] = jnp.zeros_like(state)

    later, strict = _order_masks(c, rev)
    gcum = _scan_cumsum(later.astype(F32), sm)
    gcum_t = [gcum[b].T for b in range(bsz)]
    end = 0 if rev else c - 1
    d = 1 if rev else 0
    g_col, g_row, beta = [], [], []
    for h in range(GDN_HEADS):
        col = d * GDN_HEADS + h
        bcol = 2 * GDN_HEADS + col
        for b in range(bsz):
            g_col.append(gcum[b][:, col:col + 1])
            g_row.append(gcum_t[b][col:col + 1, :])
            beta.append(sm[b][:, bcol:bcol + 1])
    gc, gr, beta = jnp.stack(g_col), jnp.stack(g_row), jnp.stack(beta)
    dec = jnp.where(later, jnp.exp(jnp.where(later, gc - gr, 0.0)), 0.0)
    qg, kg, vg = (_heads_to_batch(t, GDN_HEADS) for t in (q, k, v))
    kb = kg * beta
    scores = _bmm_nt(jnp.concatenate([kb, qg], axis=1), kg)
    a_mat = jnp.where(strict, scores[:, :c] * dec, 0.0)
    attn = scores[:, c:] * dec
    t_mat = _tri_inverse(-a_mat, _bmm)
    eg = jnp.exp(gc)
    wu = _bmm(t_mat, jnp.concatenate([kb * eg, vg * beta], axis=-1))
    w, u = wu[:, :, :GDN_HEAD_DIM], wu[:, :, GDN_HEAD_DIM:]
    g_end = gc[:, end:end + 1, :]
    s_old = state[...]
    ws = _bmm(jnp.concatenate([w, qg * eg], axis=1), s_old)
    v_new = u - ws[:, :c]
    o = ws[:, c:] + _bmm(attn, v_new)
    state[...] = s_old * jnp.exp(g_end) + _bmm_tn(kg * jnp.exp(g_end - gc), v_new)
    o = _batch_to_heads(o, GDN_HEADS)
    if not last:
        return [o]
    z, o_prev = vals[4], vals[5]
    norm_w = sh[0]
    o = o + o_prev
    cols = []
    for h in range(GDN_HEADS):
        t = o[:, :, h * GDN_HEAD_DIM:(h + 1) * GDN_HEAD_DIM]
        cols.append(t * lax.rsqrt(jnp.mean(t * t, -1, keepdims=True) + 1e-6) * norm_w[...])
    return [jnp.concatenate(cols, -1) * (z * jax.nn.sigmoid(z))]


def _gdn_params(a_log, dt_bias, n_sm):
    pad = n_sm - 2 * GDN_HEADS
    a_neg = jnp.pad(-jnp.exp(a_log.astype(F32)).reshape(1, -1), ((0, 0), (0, pad)))
    return a_neg, jnp.pad(dt_bias.astype(F32).reshape(1, -1), ((0, 0), (0, pad)))


def _deltanet_branch(prep, z, seg, norm_w):
    bsz = z.shape[0]
    w = GDN_WIDTH
    ins = [(a, a.shape[2], 0) for a in prep]
    state = [pltpu.VMEM((GDN_HEADS * bsz, GDN_HEAD_DIM, GDN_HEAD_DIM), F32)]
    (o_bwd,) = _chunk_call_all(functools.partial(_gdn_body, rev=True, last=False), ins, [], [w], [F32],
                               seg=seg, chunk=GDN_CHUNK, rev=True, scratch=state, name="gdn_bwd")
    ins = ins + [(z, w, 0), (o_bwd, w, 0)]
    (out,) = _chunk_call_all(functools.partial(_gdn_body, rev=False, last=True), ins, [norm_w.reshape(1, 1, -1)],
                             [w], [BF16], seg=seg, chunk=GDN_CHUNK, rev=False, scratch=state, name="gdn_fwd")
    return out


RWKV_HEADS = 8
RWKV_HEAD_DIM = 64
RWKV_WIDTH = RWKV_HEADS * RWKV_HEAD_DIM
RWKV_DECAY_LORA = 32
RWKV_AAA_LORA = 32
RWKV_GATE_LORA = 96
RWKV_GN_EPS = 64e-5
RWKV_CHUNK = 64
RWKV_LOG_DECAY_SCALE = -math.exp(-0.5)
RWKV_PAIRS = RWKV_WIDTH // LANE


def _rwkv_prep(rkv, sm, halo_rkv, halo_sm, sh):
    mu_main, mu_sm, k_k, k_a, r_k, w0, a0, lora_w, seg = sh
    w = RWKV_WIDTH

    def lerp(x, mu, halo):
        xs = 0.5 * (_shift_rows(x, -1, *halo) + _shift_rows(x, 1, *halo))
        return x + mu[...] * (xs - x)

    rkv = lerp(rkv, mu_main, halo_rkv)
    sm = lerp(sm, mu_sm, halo_sm)
    r, k, v = rkv[:, :w], rkv[:, w:2 * w], rkv[:, 2 * w:]
    lane = lax.broadcasted_iota(jnp.int32, sm.shape, 1)
    n_dec, n_aaa = 2 * RWKV_DECAY_LORA, 2 * RWKV_AAA_LORA
    t = jnp.where(lane < n_dec, jnp.tanh(sm), jnp.where(lane < n_dec + n_aaa, sm, jax.nn.sigmoid(sm)))
    lo = _dot(t.astype(BF16), lora_w[...])
    kk = k * k_k[...]
    kk = kk * lax.rsqrt(_seg_sum(kk * kk, seg[...]) + 1e-6)
    outs = [r, v, kk]
    k_sum = None
    for d in range(2):
        log_w = RWKV_LOG_DECAY_SCALE * jax.nn.sigmoid(w0[d:d + 1, :] + lo[:, d * w:(d + 1) * w])
        a = jax.nn.sigmoid(a0[d:d + 1, :] + lo[:, (2 + d) * w:(3 + d) * w])
        k_d = k * (1.0 + (a - 1.0) * k_a[...])
        outs += [k_d, log_w, kk * a]
        k_sum = k_d if k_sum is None else k_sum + k_d
    outs.append(lo[:, 4 * w:])
    outs.append(_seg_sum(r * k_sum * r_k[...], seg[...]) * v)
    return outs


def _head_rows(x):
    first = lax.broadcasted_iota(jnp.int32, x.shape[1:], 1) < RWKV_HEAD_DIM
    return jnp.concatenate([jnp.where(first, x, 0.0), jnp.where(first, 0.0, x)], axis=1)


def _pair_mul(x, y):
    return _bmm(x, _head_rows(y))


def _rwkv_body(n, vals, sh, scr, *, rev, last):
    r, k, v, kk, lw, b = vals[:6]
    state = scr[0]
    c = RWKV_CHUNK
    bsz = r.shape[0]

    @pl.when(n == 0)
    def _():
        state[...] = jnp.zeros_like(state)

    later, _ = _order_masks(c, rev)
    later2, strict2 = _order_masks(c, rev, reps=2)
    cum = _scan_cumsum(later.astype(F32), lw)
    end = 0 if rev else c - 1
    cum_end = cum[:, end:end + 1, :]
    e_inv = jnp.exp(-cum)
    e_tail = jnp.exp(cum_end - cum)
    to_pairs = functools.partial(_heads_to_batch, n=RWKV_PAIRS)
    r_t = to_pairs(r * jnp.exp(cum))
    a_t = to_pairs(-kk * jnp.exp(cum - lw))
    b_t, k_t = to_pairs(b * e_inv), to_pairs(k * e_inv)
    b_h, k_h = to_pairs(b * e_tail), to_pairs(k * e_tail)
    p_end = to_pairs(jnp.exp(cum_end))
    v_p = to_pairs(v)
    prods = _bmm_nt(jnp.concatenate([a_t, r_t], axis=1),
                    jnp.concatenate([_head_rows(b_t), _head_rows(k_t)], axis=1))
    a_ab = jnp.where(strict2, prods[:, :c, :LANE], 0.0)
    a_ak = jnp.where(strict2, prods[:, :c, LANE:], 0.0)
    a_rb = jnp.where(later2, prods[:, c:, :LANE], 0.0)
    a_rk = jnp.where(later2, prods[:, c:, LANE:], 0.0)
    inv = _tri_inverse(a_ab, _pair_mul)
    akv = _pair_mul(a_ak, v_p)
    wu = _bmm(inv, jnp.concatenate([_head_rows(a_t), _head_rows(akv)], axis=-1))
    w_t, u_t = wu[:, :, :LANE], wu[:, :, LANE:]
    s_old = state[...]
    ws = _bmm_nt(jnp.concatenate([w_t, r_t], axis=1), s_old)
    u = ws[:, :c] + u_t
    y = ws[:, c:] + _bmm(jnp.concatenate([a_rb, a_rk], axis=-1),
                         jnp.concatenate([_head_rows(u), _head_rows(v_p)], axis=1))
    s_new = s_old * p_end + _bmm_tn(jnp.concatenate([u, v_p], axis=1), jnp.concatenate([b_h, k_h], axis=1))
    bi = lax.broadcasted_iota(jnp.int32, (LANE, LANE), 0) < RWKV_HEAD_DIM
    bj = lax.broadcasted_iota(jnp.int32, (LANE, LANE), 1) < RWKV_HEAD_DIM
    state[...] = jnp.where(bi == bj, s_new, 0.0)
    y = _batch_to_heads(y, RWKV_PAIRS)
    if not last:
        return [y]
    y_prev, g, bonus = vals[6:9]
    ln_w, ln_b, seg = sh
    w = RWKV_WIDTH
    y = (y + y_prev).reshape(bsz * c, w)
    inv_n = 1.0 / RWKV_HEAD_DIM
    mu = _seg_sum(y, seg[...]) * inv_n
    yc = y - mu
    var = _seg_sum(yc * yc, seg[...]) * inv_n
    out = (yc * lax.rsqrt(var + RWKV_GN_EPS)) * ln_w[...] + ln_b[...]
    return [(out.reshape(bsz, c, w) + bonus) * g]


def _rwkv_params(n_sm, mu, w0, w_up, a0, a_up, g_up, k_k, k_a, r_k):
    w = RWKV_WIDTH
    n_used = 2 * RWKV_DECAY_LORA + 2 * RWKV_AAA_LORA + RWKV_GATE_LORA
    mu_main = mu[:3 * w].reshape(1, -1)
    mu_sm = jnp.pad(mu[3 * w:], (0, n_sm - n_used)).reshape(1, -1)
    lora_w = jnp.zeros((n_sm, 5 * w), F32)
    for d in range(2):
        lo = d * RWKV_DECAY_LORA
        lora_w = lora_w.at[lo:lo + RWKV_DECAY_LORA, d * w:(d + 1) * w].set(w_up[d])
        lo = 2 * RWKV_DECAY_LORA + d * RWKV_AAA_LORA
        lora_w = lora_w.at[lo:lo + RWKV_AAA_LORA, (2 + d) * w:(3 + d) * w].set(a_up[d])
    lo = 2 * RWKV_DECAY_LORA + 2 * RWKV_AAA_LORA
    lora_w = lora_w.at[lo:lo + RWKV_GATE_LORA, 4 * w:].set(g_up).astype(BF16)
    head = jnp.arange(w) // RWKV_HEAD_DIM
    seg = (head[:, None] == head[None, :]).astype(BF16)
    row = lambda t: t.reshape(1, -1)
    return [mu_main, mu_sm, row(k_k), row(k_a), row(r_k), w0, a0, lora_w, seg]


def _rwkv7_branch(prep, seg_len, seg, ln_w, ln_b):
    w = RWKV_WIDTH
    bsz = prep[0].shape[0]
    row = lambda t: t.reshape(1, -1)
    pr = [(a, w, 0) for a in prep]
    state = [pltpu.VMEM((RWKV_PAIRS * bsz, LANE, LANE), F32)]
    (y_bwd,) = _chunk_call_all(functools.partial(_rwkv_body, rev=True, last=False),
                               [pr[0], pr[6], pr[1], pr[2], pr[7], pr[8]], [], [w], [F32],
                               seg=seg_len, chunk=RWKV_CHUNK, rev=True, scratch=state, name="rwkv_bwd")
    ins = [pr[0], pr[3], pr[1], pr[2], pr[4], pr[5], (y_bwd, w, 0), pr[9], pr[10]]
    (out,) = _chunk_call_all(functools.partial(_rwkv_body, rev=False, last=True), ins,
                             [row(ln_w), row(ln_b), seg], [w], [BF16],
                             seg=seg_len, chunk=RWKV_CHUNK, rev=False, scratch=state, name="rwkv_fwd")
    return out


def _cd_proj_body(ins, mod_ref, sh, scr, halo, *, si):
    h = ins[0][0]
    rows = h.shape[0]
    shift, scale, _ = _mod_rows(mod_ref, si)
    w_qkv, w_z, w_gsm, w_rkv, w_wsm, conv_w, a_neg, dt_bias = sh[:8]
    xb = (h * (1.0 + scale) + shift).astype(BF16)
    if halo is None:
        xb_ext = xb
    else:
        prev_r, next_r, has_prev, has_next = halo
        h_ext = jnp.concatenate([prev_r[0][0], h, next_r[0][0]], 0)
        xb_ext = (h_ext * (1.0 + scale) + shift).astype(BF16)

    def with_halo(w):
        p = _dot(xb_ext, w[...])
        if halo is None:
            zero = jnp.zeros((SUBLANE, p.shape[1]), F32)
            return p, (zero, zero)
        return (p[SUBLANE:SUBLANE + rows],
                (jnp.where(has_prev, p[:SUBLANE], 0.0), jnp.where(has_next, p[SUBLANE + rows:], 0.0)))

    qkv, halo_qkv = with_halo(w_qkv)
    rkv, halo_rkv = with_halo(w_rkv)
    wsm, halo_wsm = with_halo(w_wsm)
    z = _dot(xb, w_z[...])
    gsm = _dot(xb, w_gsm[...])
    return ([z] + _gdn_prep(qkv, gsm, *halo_qkv, conv_w, a_neg, dt_bias)
            + _rwkv_prep(rkv, wsm, halo_rkv, halo_wsm, sh[8:]))


def _pad_cols(w, n):
    return jnp.pad(w, ((0, 0), (0, n - w.shape[1])))


def _mixer_cd(h_lat, h_ctx, mod, w_in, w_out, conv_w, a_log, dt_bias, norm_w, mu, w0, w_up, a0, a_up, g_up,
              k_k, k_a, r_k, ln_w, ln_b):
    n_qkv, n_gsm = 3 * GDN_WIDTH, 4 * GDN_HEADS
    n_rkv = 3 * RWKV_WIDTH
    lo_z, lo_g = n_qkv, n_qkv + GDN_WIDTH
    lo_r = lo_g + n_gsm
    lo_s = lo_r + n_rkv
    n_gpad, n_spad = LANE, 2 * LANE
    ws = [w_in[:, :n_qkv], w_in[:, lo_z:lo_g], _pad_cols(w_in[:, lo_g:lo_r], n_gpad),
          w_in[:, lo_r:lo_s], _pad_cols(w_in[:, lo_s:], n_spad)]
    seg = (h_lat.shape[1], h_ctx.shape[1])
    rw = _rwkv_params(n_spad, mu, w0, w_up, a0, a_up, g_up, k_k, k_a, r_k)
    shared = [w.astype(BF16) for w in ws] + [conv_w] + list(_gdn_params(a_log, dt_bias, n_gpad)) + rw
    gw, rww = GDN_WIDTH, RWKV_WIDTH
    widths = [gw, gw, gw, gw, n_gpad] + [rww] * 11
    outs = _token_call(functools.partial(_cd_proj_body, si=3), [h_lat], [h_ctx], shared, widths,
                       [F32] * len(widths), mod=mod, with_ctx=True, name="mixer_in_proj_prep",
                       n_halo=1, joint_out=True)
    d = _deltanet_branch(outs[1:5], outs[0], seg, norm_w)
    r = _rwkv7_branch(outs[5:], seg, rw[-1], ln_w, ln_b)
    return [d, r], [d, r], [w_out[:GDN_WIDTH].astype(BF16), w_out[GDN_WIDTH:].astype(BF16)]


def kernel(x, c, ctx, c_ctx, ada_w, ada_b, ffn_w_in, ffn_w_out, ln_g, ln_b, ab_w_in, ab_w_out, s5_lam_re, s5_lam_im, s5_log_dt, s5_b_re, s5_b_im, s5_c_re, s5_c_im, s5_d, s5_glu_w, s5_glu_b, ret_log_rate, cd_w_in, cd_w_out, gdn_conv_w, gdn_a_log, gdn_dt_bias, gdn_norm_w, rwkv_mu, rwkv_w0, rwkv_w_up, rwkv_a0, rwkv_a_up, rwkv_g_up, rwkv_k_k, rwkv_k_a, rwkv_r_k, rwkv_ln_w, rwkv_ln_b):
    mods = _adaln(c, c_ctx, ada_w, ada_b)
    h_lat, h_ctx = x, ctx
    for i in range(DEPTH):
        mod = mods[i]
        j = i // 2
        keep_ctx = i < DEPTH - 1
        h_lat, h_ctx = _ffn(h_lat, h_ctx, mod, 0, ffn_w_in[i, 0].astype(BF16), ffn_w_out[i, 0].astype(BF16),
                            ln_g[i, 0], ln_b[i, 0])
        if i % 2 == 0:
            s5_ops = _s5_operators(s5_lam_re[j], s5_lam_im[j], s5_log_dt[j], s5_b_re[j], s5_b_im[j],
                                   s5_c_re[j], s5_c_im[j])
            ms_lat, ms_ctx, w_out = _mixer_ab(h_lat, h_ctx, mod, ab_w_in[j], ab_w_out[j], s5_ops, s5_d[j],
                                              s5_glu_w[j], s5_glu_b[j], ret_log_rate[j])
        else:
            ms_lat, ms_ctx, w_out = _mixer_cd(
                h_lat, h_ctx, mod, cd_w_in[j], cd_w_out[j], gdn_conv_w[j], gdn_a_log[j], gdn_dt_bias[j],
                gdn_norm_w[j], rwkv_mu[j], rwkv_w0[j], rwkv_w_up[j], rwkv_a0[j], rwkv_a_up[j], rwkv_g_up[j],
                rwkv_k_k[j], rwkv_k_a[j], rwkv_r_k[j], rwkv_ln_w[j], rwkv_ln_b[j])
        h_lat, h_ctx = _ffn(h_lat, h_ctx, mod, 6, ffn_w_in[i, 1].astype(BF16), ffn_w_out[i, 1].astype(BF16),
                            ln_g[i, 2], ln_b[i, 2], with_ctx=keep_ctx,
                            mix=(ms_lat, ms_ctx, w_out, 5, ln_g[i, 1], ln_b[i, 1]))
    return h_lat
```
